```python
import math
import jax, jax.numpy as jnp
from jax import lax
import numpy as np

D_MODEL = 2048
BATCH = 16
SEQ = 256
DEPTH = 4
DEC_BATCH = 2
DEC_SEQ = 2048
PAST_LEN = 256

GRID_W = 64
EPS = 1e-6
ROPE_BASE = 10000.0
Q_BLOCK = 128

MLA_HEADS = 8
QK_NOPE = 128
QK_ROPE = 64
V_HEAD = 128
Q_LORA = 512
KV_LORA = 512
MLA_W = MLA_HEADS * V_HEAD
SGU_CHUNK = 128
SGU_GROUPS = 8
SGU_W = 1024
SGU_GC = SGU_W // SGU_GROUPS
RET_HEADS = 8
RET_DK = 128
RET_DV = 128
RET_W = RET_HEADS * RET_DV
RET_CHUNK = 128
N_BRANCH = 3

IN_SPLITS = (Q_LORA, KV_LORA, QK_ROPE, MLA_W,
             SGU_W, SGU_W, SGU_W,
             RET_HEADS * RET_DK, RET_HEADS * RET_DK, RET_W, RET_W,
             N_BRANCH * D_MODEL)
IN_COLS = sum(IN_SPLITS)

kernel_name = "hybrid_mla_sgu_retention_diffusion_step"


def rms_norm(x, g):
    xf = x.astype(jnp.float32)
    y = xf * lax.rsqrt(jnp.mean(xf * xf, axis=-1, keepdims=True) + EPS)
    return (y * g.astype(jnp.float32)).astype(x.dtype)


def axial_rope(n_tok):
    rows = n_tok // GRID_W
    row = jnp.repeat(jnp.arange(rows, dtype=jnp.float32), GRID_W)
    col = jnp.tile(jnp.arange(GRID_W, dtype=jnp.float32), rows)
    n_freq = QK_ROPE // 4
    inv = ROPE_BASE ** (-jnp.arange(n_freq, dtype=jnp.float32) / n_freq)
    ang_r = (row[:, None] * inv)[:, None, :]
    ang_c = (col[:, None] * inv)[:, None, :]
    return (jnp.cos(ang_r), jnp.sin(ang_r), jnp.cos(ang_c), jnp.sin(ang_c))


def rotate_pairs(x, cos, sin):
    x1, x2 = jnp.split(x, 2, axis=-1)
    return jnp.concatenate([x1 * cos - x2 * sin, x2 * cos + x1 * sin], axis=-1)


def apply_axial_rope(x, rope):
    cr, sr, cc, sc = rope
    xf = x.astype(jnp.float32)
    half = QK_ROPE // 2
    out = jnp.concatenate([rotate_pairs(xf[..., :half], cr, sr),
                           rotate_pairs(xf[..., half:], cc, sc)], axis=-1)
    return out.astype(x.dtype)


def block_attention(q, k, v):
    b, lq, h, dk = q.shape
    scale = dk ** -0.5
    qb = q.reshape(b, lq // Q_BLOCK, Q_BLOCK, h, dk).transpose(1, 0, 2, 3, 4)

    def one(qblk):
        s = jnp.einsum('bqhd,bkhd->bhqk', qblk, k).astype(jnp.float32) * scale
        p = jax.nn.softmax(s, axis=-1).astype(v.dtype)
        return jnp.einsum('bhqk,bkhe->bqhe', p, v)

    o = lax.map(one, qb)
    return o.transpose(1, 0, 2, 3, 4).reshape(b, lq, h, v.shape[-1])


def mla_query(cq, g_q, w_uq, rope):
    b, L, _ = cq.shape
    q = (rms_norm(cq, g_q) @ w_uq).reshape(b, L, MLA_HEADS, QK_NOPE + QK_ROPE)
    if rope is not None:
        q = jnp.concatenate([q[..., :QK_NOPE], apply_axial_rope(q[..., QK_NOPE:], rope)], axis=-1)
    return q


def mla_keys(ckv_n, k_rope, w_ukv):
    b, L, _ = ckv_n.shape
    kv = (ckv_n @ w_ukv).reshape(b, L, MLA_HEADS, QK_NOPE + V_HEAD)
    k = jnp.concatenate([kv[..., :QK_NOPE],
                         jnp.broadcast_to(k_rope[:, :, None, :], (b, L, MLA_HEADS, QK_ROPE))], axis=-1)
    return k, kv[..., QK_NOPE:]


def chunk_sgu(u, v, g_sgu, w_s, b_s):
    b, L, _ = u.shape
    vn = rms_norm(v, g_sgu).reshape(b, L // SGU_CHUNK, SGU_CHUNK, SGU_GROUPS, SGU_GC)
    s = jnp.einsum('gpq,bnqgc->bnpgc', w_s, vn) + b_s[None, None, :, :, None]
    return u * s.reshape(b, L, SGU_W).astype(u.dtype)


def retention_scan(q, k, v, log_g, r0, inclusive):
    b, L, h, _ = q.shape
    C = RET_CHUNK
    n = L // C
    idx = jnp.arange(C, dtype=jnp.float32)
    diff = idx[:, None] - idx[None, :]
    mask = (diff >= 0) if inclusive else (diff > 0)
    dec = jnp.where(mask[None], jnp.exp(jnp.where(mask, diff, 0.0)[None] * log_g[:, None, None]), 0.0)
    xi = jnp.exp((idx[:, None] + 1.0) * log_g[None, :])[None, :, :, None]
    zeta = jnp.exp((C - 1.0 - idx)[:, None] * log_g[None, :])[None, :, :, None]
    g_c = jnp.exp(C * log_g)[None, :, None, None]

    def to_chunks(t):
        return t.reshape(b, n, C, h, t.shape[-1]).transpose(1, 0, 2, 3, 4)

    def step(r, inp):
        qc, kc, vc = inp
        att = jnp.einsum('bihd,bjhd->bhij', qc, kc) * dec[None]
        o = jnp.einsum('bhij,bjhe->bihe', att, vc) + jnp.einsum('bihd,bhde->bihe', qc, r) * xi
        r = g_c * r + jnp.einsum('bjhd,bjhe->bhde', kc * zeta, vc)
        return r, o

    r, o = lax.scan(step, r0, (to_chunks(q), to_chunks(k), to_chunks(v)))
    return o.transpose(1, 0, 2, 3, 4).reshape(b, L, h, v.shape[-1]), r


def bidir_retention(q, k, v, log_g2, r0_f, r0_b):
    o_f, r_f = retention_scan(q, k, v, log_g2[0], r0_f, True)
    flip = lambda t: jnp.flip(t, axis=1)
    o_b, r_b = retention_scan(flip(q), flip(k), flip(v), log_g2[1], r0_b, False)
    return o_f + flip(o_b), r_f, r_b


def head_layer_norm(o, g):
    b, L = o.shape[:2]
    mu = jnp.mean(o, axis=-1, keepdims=True)
    var = jnp.mean(jnp.square(o - mu), axis=-1, keepdims=True)
    y = (o - mu) * lax.rsqrt(var + EPS)
    return y.reshape(b, L, RET_W) * g.astype(jnp.float32)


def layer_forward(x, cond, lw, rope, ctx):
    b, L, _ = x.shape
    mod = jax.nn.silu(cond) @ lw['w_mod'] + lw['b_mod']
    shift, scale, gate = jnp.split(mod[:, None, :], 3, axis=-1)
    h = rms_norm(x, lw['g_pre']) * (1 + scale) + shift
    z = h @ lw['w_in']
    offs = tuple(int(o) for o in np.cumsum(IN_SPLITS)[:-1])
    (cq, ckv_raw, krope, gp_mla, u, v_s, gp_sgu, rq, rk, rv, gp_ret, merge) = jnp.split(z, offs, axis=-1)

    ckv_n = rms_norm(ckv_raw, lw['g_kv'])
    if rope is not None:
        krope = apply_axial_rope(krope[:, :, None, :], rope)[:, :, 0, :]
    q = mla_query(cq, lw['g_q'], lw['w_uq'], rope)
    k, vv = mla_keys(ckv_n, krope, lw['w_ukv'])
    if ctx is not None:
        kc, vc = mla_keys(ctx[0], ctx[1], lw['w_ukv'])
        k = jnp.concatenate([kc, k], axis=1)
        vv = jnp.concatenate([vc, vv], axis=1)
    o_mla = block_attention(q, k, vv).reshape(b, L, MLA_W)

    o_sgu = chunk_sgu(u, v_s, lw['g_sgu'], lw['w_sgu'], lw['b_sgu'])

    log_g2 = jax.nn.log_sigmoid(lw['ret_decay'].astype(jnp.float32))
    rqf = rq.reshape(b, L, RET_HEADS, RET_DK).astype(jnp.float32)
    rkf = rk.reshape(b, L, RET_HEADS, RET_DK).astype(jnp.float32) * (RET_DK ** -0.5)
    rvf = rv.reshape(b, L, RET_HEADS, RET_DV).astype(jnp.float32)
    if ctx is None:
        r0_f = jnp.zeros((b, RET_HEADS, RET_DK, RET_DV), jnp.float32)
        r0_b = r0_f
    else:
        r0_f = ctx[2].astype(jnp.float32)
        r0_b = ctx[3].astype(jnp.float32)
    o_ret, r_f, r_b = bidir_retention(rqf, rkf, rvf, log_g2, r0_f, r0_b)
    o_ret = head_layer_norm(o_ret, lw['g_ret']).astype(x.dtype)

    y_mla = (jax.nn.silu(gp_mla) * o_mla) @ lw['w_br_mla']
    y_sgu = (jax.nn.silu(gp_sgu) * o_sgu) @ lw['w_br_sgu']
    y_ret = (jax.nn.silu(gp_ret) * o_ret) @ lw['w_br_ret']
    m_mla, m_sgu, m_ret = jnp.split(jax.nn.sigmoid(merge), N_BRANCH, axis=-1)
    y = (m_mla * y_mla + m_sgu * y_sgu + m_ret * y_ret) @ lw['w_out']
    x = x + gate * rms_norm(y, lw['g_post'])
    return x, (ckv_n, krope, r_f.astype(x.dtype), r_b.astype(x.dtype))


def setup_inputs(seed: int = 0) -> dict:
    key = jax.random.key(seed)
    ks = jax.random.split(key, 25)
    f32 = jnp.float32
    nrm = lambda k, shape, s: jax.random.normal(k, shape, f32) * s
    heads = jnp.arange(RET_HEADS, dtype=f32)
    one_minus = 2.0 ** (-5.0 - heads)
    decay_logit = jnp.log1p(-one_minus) - jnp.log(one_minus)
    return {
        "x_prompt": nrm(ks[0], (BATCH, SEQ, D_MODEL), 1.0),
        "x_sample": nrm(ks[1], (DEC_BATCH, DEC_SEQ, D_MODEL), 1.0),
        "cache_ckv": nrm(ks[2], (DEC_BATCH, DEPTH, PAST_LEN, KV_LORA), 1.0),
        "cache_krope": nrm(ks[3], (DEC_BATCH, DEPTH, PAST_LEN, QK_ROPE), 1.0),
        "state_ret": nrm(ks[4], (DEC_BATCH, DEPTH, 2, RET_HEADS, RET_DK, RET_DV), 0.5),
        "c": nrm(ks[5], (DEC_BATCH, D_MODEL), 1.0),
        "c_ctx": nrm(ks[6], (D_MODEL,), 1.0),
        "w_mod": nrm(ks[7], (DEPTH, D_MODEL, 3 * D_MODEL), 0.5 * D_MODEL ** -0.5),
        "b_mod": nrm(ks[8], (DEPTH, 3 * D_MODEL), 0.01),
        "g_pre": 1.0 + nrm(ks[9], (DEPTH, D_MODEL), 0.1),
        "g_post": 1.0 + nrm(ks[10], (DEPTH, D_MODEL), 0.1),
        "w_in": nrm(ks[11], (DEPTH, D_MODEL, IN_COLS), D_MODEL ** -0.5),
        "g_q": 1.0 + nrm(ks[12], (DEPTH, Q_LORA), 0.1),
        "g_kv": 1.0 + nrm(ks[13], (DEPTH, KV_LORA), 0.1),
        "w_uq": nrm(ks[14], (DEPTH, Q_LORA, MLA_HEADS * (QK_NOPE + QK_ROPE)), Q_LORA ** -0.5),
        "w_ukv": nrm(ks[15], (DEPTH, KV_LORA, MLA_HEADS * (QK_NOPE + V_HEAD)), KV_LORA ** -0.5),
        "g_sgu": 1.0 + nrm(ks[16], (DEPTH, SGU_W), 0.1),
        "w_sgu": nrm(ks[17], (DEPTH, SGU_GROUPS, SGU_CHUNK, SGU_CHUNK), SGU_CHUNK ** -0.5),
        "b_sgu": nrm(ks[18], (DEPTH, SGU_CHUNK, SGU_GROUPS), 0.1),
        "ret_decay": decay_logit[None, None, :] + nrm(ks[19], (DEPTH, 2, RET_HEADS), 0.1),
        "g_ret": 1.0 + nrm(ks[20], (DEPTH, RET_W), 0.1),
        "w_br_mla": nrm(ks[21], (DEPTH, MLA_W, D_MODEL), MLA_W ** -0.5),
        "w_br_sgu": nrm(ks[22], (DEPTH, SGU_W, D_MODEL), SGU_W ** -0.5),
        "w_br_ret": nrm(ks[23], (DEPTH, RET_W, D_MODEL), RET_W ** -0.5),
        "w_out": nrm(ks[24], (DEPTH, D_MODEL, D_MODEL), D_MODEL ** -0.5),
    }


def reference(x_prompt, x_sample, cache_ckv, cache_krope, state_ret, c, c_ctx,
              w_mod, b_mod, g_pre, g_post, w_in, g_q, g_kv, w_uq, w_ukv,
              g_sgu, w_sgu, b_sgu, ret_decay, g_ret, w_br_mla, w_br_sgu, w_br_ret, w_out):
    def layer_weights(l):
        return dict(w_mod=w_mod[l], b_mod=b_mod[l], g_pre=g_pre[l], g_post=g_post[l], w_in=w_in[l],
                    g_q=g_q[l], g_kv=g_kv[l], w_uq=w_uq[l], w_ukv=w_ukv[l], g_sgu=g_sgu[l],
                    w_sgu=w_sgu[l], b_sgu=b_sgu[l], ret_decay=ret_decay[l], g_ret=g_ret[l],
                    w_br_mla=w_br_mla[l], w_br_sgu=w_br_sgu[l], w_br_ret=w_br_ret[l], w_out=w_out[l])

    y_prompt = x_prompt
    ckvs, kropes, rets = [], [], []
    for l in range(DEPTH):
        y_prompt, (ckv_n, kr, r_f, r_b) = layer_forward(y_prompt, c_ctx[None, :], layer_weights(l), None, None)
        ckvs.append(ckv_n)
        kropes.append(kr)
        rets.append(jnp.stack([r_f, r_b], axis=1))
    new_ckv = jnp.stack(ckvs, axis=1)
    new_krope = jnp.stack(kropes, axis=1)
    new_ret = jnp.stack(rets, axis=1)

    rope = axial_rope(x_sample.shape[1])
    y_sample = x_sample
    for l in range(DEPTH):
        ctx = (cache_ckv[:, l], cache_krope[:, l], state_ret[:, l, 0], state_ret[:, l, 1])
        y_sample, _ = layer_forward(y_sample, c, layer_weights(l), rope, ctx)

    return (y_prompt, y_sample, new_ckv, new_krope, new_ret)
```

```python
import functools

import jax
import jax.numpy as jnp
from jax import lax
from jax.experimental import pallas as pl
from jax.experimental.pallas import tpu as pltpu

F32 = jnp.float32
BF16 = jnp.bfloat16

D_MODEL = 2048
BATCH = 16
SEQ = 256
DEPTH = 4
DEC_BATCH = 2
DEC_SEQ = 2048
PAST_LEN = 256
GRID_W = 64
EPS = 1e-6
ROPE_BASE = 10000.0

MLA_HEADS = 8
QK_NOPE = 128
QK_ROPE = 64
V_HEAD = 128
Q_LORA = 512
KV_LORA = 512
MLA_W = MLA_HEADS * V_HEAD
SGU_CHUNK = 128
SGU_GROUPS = 8
SGU_W = 1024
RET_HEADS = 8
RET_DK = 128
RET_DV = 128
RET_W = RET_HEADS * RET_DV
RET_CHUNK = 128
IN_COLS = 15424

LANE = 128
N_P = BATCH * SEQ
N_S = DEC_BATCH * DEC_SEQ
N_TOK = N_P + N_S
COND_ROWS = 8
HEAD_SLOT = 2 * LANE
KR_COL = Q_LORA + KV_LORA
Z_OFF = KR_COL + QK_ROPE
Z_BLK = 1024
N_ZBLK = 15
ZB_GP_MLA, ZB_U, ZB_VS, ZB_GP_SGU, ZB_RQ, ZB_RK, ZB_RV, ZB_GP_RET, ZB_MERGE = 1, 2, 3, 4, 5, 6, 7, 8, 9
VMEM_LIMIT = 56 * 1024 * 1024


def _params(n_axes, vmem=VMEM_LIMIT):
    return pltpu.CompilerParams(dimension_semantics=("arbitrary",) * n_axes, vmem_limit_bytes=vmem)


def _rms(x, g):
    return x * lax.rsqrt(jnp.mean(x * x, axis=-1, keepdims=True) + EPS) * g


def _silu(x):
    return x * jax.nn.sigmoid(x)


def _cond_row(tile, tm):
    return jnp.where(tile < N_P // tm, 0, 1 + (tile - N_P // tm) // (DEC_SEQ // tm))


def _cast_rows(dst_ref, src, n_rows, step=256):
    for r in range(0, n_rows, step):
        dst_ref[r:r + step, :] = src(r, step).astype(BF16)


def _mod_kernel(cond_ref, w_ref, b_ref, o_ref):
    s = _silu(cond_ref[...]).astype(BF16)
    o_ref[0] = jnp.dot(s, w_ref[0].astype(BF16), preferred_element_type=F32) + b_ref[0]


def _modulation(cond8, w_mod, b_mod):
    tn = 1024
    return pl.pallas_call(
        _mod_kernel,
        grid=(DEPTH, 3 * D_MODEL // tn),
        in_specs=[pl.BlockSpec((COND_ROWS, D_MODEL), lambda l, n: (0, 0)),
                  pl.BlockSpec((1, D_MODEL, tn), lambda l, n: (l, 0, n)),
                  pl.BlockSpec((1, 1, tn), lambda l, n: (l, 0, n))],
        out_specs=pl.BlockSpec((1, COND_ROWS, tn), lambda l, n: (l, 0, n)),
        out_shape=jax.ShapeDtypeStruct((DEPTH, COND_ROWS, 3 * D_MODEL), F32),
        compiler_params=_params(2),
        name="modulation",
    )(cond8, w_mod, b_mod.reshape(DEPTH, 1, 3 * D_MODEL))


def _prenorm_kernel(x_ref, g_ref, mod_ref, h_ref):
    y = _rms(x_ref[...], g_ref[...])
    h_ref[...] = (y * (1.0 + mod_ref[0, 1:2, :]) + mod_ref[0, 0:1, :]).astype(BF16)


def _prenorm(x, g_pre_l, mod3, l):
    tm = 512
    return pl.pallas_call(
        _prenorm_kernel,
        grid=(N_TOK // tm,),
        in_specs=[pl.BlockSpec((tm, D_MODEL), lambda i: (i, 0)),
                  pl.BlockSpec((1, D_MODEL), lambda i: (0, 0)),
                  pl.BlockSpec((1, 3, D_MODEL), lambda i: (l * COND_ROWS + _cond_row(i, tm), 0, 0))],
        out_specs=pl.BlockSpec((tm, D_MODEL), lambda i: (i, 0)),
        out_shape=jax.ShapeDtypeStruct((N_TOK, D_MODEL), BF16),
        compiler_params=_params(1),
        name="prenorm",
    )(x, g_pre_l, mod3)


def _inproj_kernel(h_ref, wm_ref, we_ref, o_ref, wbf_ref):
    j = pl.program_id(0)
    i = pl.program_id(1)

    @pl.when(jnp.logical_and(i == 0, j == 0))
    def _():
        _cast_rows(wbf_ref, lambda r, s: wm_ref[0, r:r + s, :], D_MODEL)

    @pl.when(jnp.logical_and(i == 0, j > 0))
    def _():
        _cast_rows(wbf_ref,
                   lambda r, s: jnp.concatenate([wm_ref[0, r:r + s, QK_ROPE:], we_ref[0, r:r + s, :QK_ROPE]], axis=1),
                   D_MODEL)

    acc = jnp.dot(h_ref[...], wbf_ref[...], preferred_element_type=F32)
    is_silu = jnp.logical_or(jnp.logical_or(j == ZB_GP_MLA, j == ZB_GP_SGU), j == ZB_GP_RET)
    is_sig = j >= ZB_MERGE

    @pl.when(is_silu)
    def _():
        o_ref[...] = _silu(acc)

    @pl.when(is_sig)
    def _():
        o_ref[...] = jax.nn.sigmoid(acc)

    @pl.when(jnp.logical_not(jnp.logical_or(is_silu, is_sig)))
    def _():
        o_ref[...] = acc


def _inproj(h, w_in, l):
    tm = 512
    ext_per_blk = Z_BLK // LANE
    return pl.pallas_call(
        _inproj_kernel,
        grid=(N_ZBLK, N_TOK // tm),
        in_specs=[pl.BlockSpec((tm, D_MODEL), lambda j, i: (i, 0)),
                  pl.BlockSpec((1, D_MODEL, Z_BLK), lambda j, i: (l, 0, j)),
                  pl.BlockSpec((1, D_MODEL, LANE), lambda j, i: (l, 0, (j + 1) * ext_per_blk))],
        out_specs=pl.BlockSpec((tm, Z_BLK), lambda j, i: (i, j)),
        out_shape=jax.ShapeDtypeStruct((N_TOK, N_ZBLK * Z_BLK), F32),
        scratch_shapes=[pltpu.VMEM((D_MODEL, Z_BLK), BF16)],
        compiler_params=_params(2),
        name="inproj",
    )(h, w_in, w_in)


def _rope(x, c, s1, s2):
    return x * c + pltpu.roll(x, LANE - QK_ROPE // 4, 1) * s1 + pltpu.roll(x, QK_ROPE // 4, 1) * s2


def _qkv_kernel(z_ref, h_ref, wkr_ref, wuq_ref, wukv_ref, gq_ref, gkv_ref, *rest, rope):
    if rope:
        c_ref, s1_ref, s2_ref, q_ref, k_ref, v_ref, wkr_bf, wuq_bf, wukv_bf = rest
    else:
        q_ref, k_ref, v_ref, ckvn_ref, kr_ref, wkr_bf, wuq_bf, wukv_bf = rest

    @pl.when(pl.program_id(0) == 0)
    def _():
        _cast_rows(wkr_bf, lambda r, s: wkr_ref[0, r:r + s, :], D_MODEL)
        _cast_rows(wuq_bf, lambda r, s: wuq_ref[0, r:r + s, :], Q_LORA)
        _cast_rows(wukv_bf, lambda r, s: wukv_ref[0, r:r + s, :], KV_LORA)

    cq = z_ref[:, :Q_LORA]
    ckv = z_ref[:, Q_LORA:]
    q = jnp.dot(_rms(cq, gq_ref[...]).astype(BF16), wuq_bf[...], preferred_element_type=F32)
    ckvn = _rms(ckv, gkv_ref[...])
    kv = jnp.dot(ckvn.astype(BF16), wukv_bf[...], preferred_element_type=F32)
    kr = jnp.dot(h_ref[...], wkr_bf[...], preferred_element_type=F32)
    lane = lax.broadcasted_iota(jnp.int32, kr.shape, 1)
    kr = jnp.where(lane < QK_ROPE, kr, 0.0)
    if rope:
        c, s1, s2 = c_ref[...], s1_ref[...], s2_ref[...]
        kr = _rope(kr, c, s1, s2)
    else:
        ckvn_ref[0, 0] = ckvn
        kr_ref[0, 0] = kr[:, :QK_ROPE]
    kr_bf = kr.astype(BF16)
    for hd in range(MLA_HEADS):
        lo = hd * HEAD_SLOT
        q_ref[:, lo:lo + LANE] = q[:, lo:lo + LANE].astype(BF16)
        qr = q[:, lo + LANE:lo + HEAD_SLOT]
        if rope:
            qr = _rope(qr, c, s1, s2)
        q_ref[:, lo + LANE:lo + HEAD_SLOT] = qr.astype(BF16)
        k_ref[:, lo:lo + LANE] = kv[:, lo:lo + LANE].astype(BF16)
        k_ref[:, lo + LANE:lo + HEAD_SLOT] = kr_bf
        v_ref[:, hd * LANE:(hd + 1) * LANE] = kv[:, lo + LANE:lo + HEAD_SLOT].astype(BF16)


def _qkv(z, h, w_in, w_uq_pad, w_ukv, g_q_l, g_kv_l, rope_tabs, l, group):
    tm = 256
    n_tok = N_P if group == 0 else N_S
    t0 = 0 if group == 0 else N_P // tm
    rope = group == 1
    in_specs = [pl.BlockSpec((tm, Z_BLK), lambda i: (i + t0, 0)),
                pl.BlockSpec((tm, D_MODEL), lambda i: (i + t0, 0)),
                pl.BlockSpec((1, D_MODEL, LANE), lambda i: (l, 0, KR_COL // LANE)),
                pl.BlockSpec((1, Q_LORA, MLA_HEADS * HEAD_SLOT), lambda i: (l, 0, 0)),
                pl.BlockSpec((1, KV_LORA, MLA_HEADS * HEAD_SLOT), lambda i: (l, 0, 0)),
                pl.BlockSpec((1, Q_LORA), lambda i: (0, 0)),
                pl.BlockSpec((1, KV_LORA), lambda i: (0, 0))]
    args = [z, h, w_in, w_uq_pad, w_ukv, g_q_l, g_kv_l]
    out_specs = [pl.BlockSpec((tm, MLA_HEADS * HEAD_SLOT), lambda i: (i, 0)),
                 pl.BlockSpec((tm, MLA_HEADS * HEAD_SLOT), lambda i: (i, 0)),
                 pl.BlockSpec((tm, MLA_W), lambda i: (i, 0))]
    out_shape = [jax.ShapeDtypeStruct((n_tok, MLA_HEADS * HEAD_SLOT), BF16),
                 jax.ShapeDtypeStruct((n_tok, MLA_HEADS * HEAD_SLOT), BF16),
                 jax.ShapeDtypeStruct((n_tok, MLA_W), BF16)]
    if rope:
        tiles_per_seq = DEC_SEQ // tm
        in_specs += [pl.BlockSpec((tm, LANE), lambda i: (i % tiles_per_seq, 0))] * 3
        args += list(rope_tabs)
    else:
        assert tm == SEQ
        out_specs += [pl.BlockSpec((1, 1, SEQ, KV_LORA), lambda i: (i, 0, 0, 0)),
                      pl.BlockSpec((1, 1, SEQ, QK_ROPE), lambda i: (i, 0, 0, 0))]
        out_shape += [jax.ShapeDtypeStruct((BATCH, 1, SEQ, KV_LORA), F32),
                      jax.ShapeDtypeStruct((BATCH, 1, SEQ, QK_ROPE), F32)]
    return pl.pallas_call(
        functools.partial(_qkv_kernel, rope=rope),
        grid=(n_tok // tm,),
        in_specs=in_specs,
        out_specs=out_specs,
        out_shape=out_shape,
        scratch_shapes=[pltpu.VMEM((D_MODEL, LANE), BF16),
                        pltpu.VMEM((Q_LORA, MLA_HEADS * HEAD_SLOT), BF16),
                        pltpu.VMEM((KV_LORA, MLA_HEADS * HEAD_SLOT), BF16)],
        compiler_params=_params(1),
        name="qkv_sample" if rope else "qkv_prompt",
    )(*args)


def _ctxkv_kernel(ckv_ref, kr_ref, w_ref, k_ref, v_ref):
    kv = jnp.dot(ckv_ref[0, 0].astype(BF16), w_ref[0].astype(BF16), preferred_element_type=F32)
    kr = jnp.concatenate([kr_ref[0, 0], jnp.zeros((PAST_LEN, LANE - QK_ROPE), F32)], axis=1).astype(BF16)
    for hd in range(MLA_HEADS):
        lo = hd * HEAD_SLOT
        k_ref[0, 0, :, lo:lo + LANE] = kv[:, lo:lo + LANE].astype(BF16)
        k_ref[0, 0, :, lo + LANE:lo + HEAD_SLOT] = kr
        v_ref[0, 0, :, hd * LANE:(hd + 1) * LANE] = kv[:, lo + LANE:lo + HEAD_SLOT].astype(BF16)


def _ctxkv(cache_ckv, cache_krope, w_ukv):
    return pl.pallas_call(
        _ctxkv_kernel,
        grid=(DEPTH, DEC_BATCH),
        in_specs=[pl.BlockSpec((1, 1, PAST_LEN, KV_LORA), lambda l, b: (b, l, 0, 0)),
                  pl.BlockSpec((1, 1, PAST_LEN, QK_ROPE), lambda l, b: (b, l, 0, 0)),
                  pl.BlockSpec((1, KV_LORA, MLA_HEADS * HEAD_SLOT), lambda l, b: (l, 0, 0))],
        out_specs=[pl.BlockSpec((1, 1, PAST_LEN, MLA_HEADS * HEAD_SLOT), lambda l, b: (l, b, 0, 0)),
                   pl.BlockSpec((1, 1, PAST_LEN, MLA_W), lambda l, b: (l, b, 0, 0))],
        out_shape=[jax.ShapeDtypeStruct((DEPTH, DEC_BATCH, PAST_LEN, MLA_HEADS * HEAD_SLOT), BF16),
                   jax.ShapeDtypeStruct((DEPTH, DEC_BATCH, PAST_LEN, MLA_W), BF16)],
        compiler_params=_params(2),
        name="ctxkv",
    )(cache_ckv, cache_krope, w_ukv)


_QK_SCALE = (QK_NOPE + QK_ROPE) ** -0.5
_NT = (((1,), (1,)), ((), ()))


def _attn_prompt_kernel(q_ref, k_ref, v_ref, gp_ref, o_ref):
    for hd in range(MLA_HEADS):
        lo = hd * HEAD_SLOT
        s = lax.dot_general(q_ref[:, lo:lo + HEAD_SLOT], k_ref[:, lo:lo + HEAD_SLOT], _NT,
                            preferred_element_type=F32) * _QK_SCALE
        e = jnp.exp(s - jnp.max(s, axis=-1, keepdims=True))
        p = (e / jnp.sum(e, axis=-1, keepdims=True)).astype(BF16)
        o = jnp.dot(p, v_ref[:, hd * LANE:(hd + 1) * LANE], preferred_element_type=F32)
        o_ref[:, hd * LANE:(hd + 1) * LANE] = (gp_ref[:, hd * LANE:(hd + 1) * LANE] * o).astype(BF16)


def _attn_prompt(q, k, v, z):
    return pl.pallas_call(
        _attn_prompt_kernel,
        grid=(BATCH,),
        in_specs=[pl.BlockSpec((SEQ, MLA_HEADS * HEAD_SLOT), lambda b: (b, 0)),
                  pl.BlockSpec((SEQ, MLA_HEADS * HEAD_SLOT), lambda b: (b, 0)),
                  pl.BlockSpec((SEQ, MLA_W), lambda b: (b, 0)),
                  pl.BlockSpec((SEQ, Z_BLK), lambda b: (b, ZB_GP_MLA))],
        out_specs=pl.BlockSpec((SEQ, MLA_W), lambda b: (b, 0)),
        out_shape=jax.ShapeDtypeStruct((N_P, MLA_W), BF16),
        compiler_params=_params(1),
        name="attn_prompt",
    )(q, k, v, z)


def _attn_sample_kernel(q_ref, k_ref, v_ref, kc_ref, vc_ref, gp_ref, o_ref):
    q = q_ref[...]
    sc = lax.dot_general(q, kc_ref[0, 0], _NT, preferred_element_type=F32) * _QK_SCALE
    sn = lax.dot_general(q, k_ref[...], _NT, preferred_element_type=F32) * _QK_SCALE
    m = jnp.maximum(jnp.max(sc, axis=-1, keepdims=True), jnp.max(sn, axis=-1, keepdims=True))
    ec = jnp.exp(sc - m)
    en = jnp.exp(sn - m)
    den = jnp.sum(ec, axis=-1, keepdims=True) + jnp.sum(en, axis=-1, keepdims=True)
    o = jnp.dot((ec / den).astype(BF16), vc_ref[0, 0], preferred_element_type=F32)
    o = o + jnp.dot((en / den).astype(BF16), v_ref[...], preferred_element_type=F32)
    o_ref[...] = (gp_ref[...] * o).astype(BF16)


def _attn_sample(q, k, v, kc, vc, z, l):
    tq = 256
    tiles_per_seq = DEC_SEQ // tq
    z_tile0 = N_P // tq
    return pl.pallas_call(
        _attn_sample_kernel,
        grid=(DEC_BATCH, MLA_HEADS, tiles_per_seq),
        in_specs=[pl.BlockSpec((tq, HEAD_SLOT), lambda b, hd, t: (b * tiles_per_seq + t, hd)),
                  pl.BlockSpec((DEC_SEQ, HEAD_SLOT), lambda b, hd, t: (b, hd)),
                  pl.BlockSpec((DEC_SEQ, LANE), lambda b, hd, t: (b, hd)),
                  pl.BlockSpec((1, 1, PAST_LEN, HEAD_SLOT), lambda b, hd, t: (l, b, 0, hd)),
                  pl.BlockSpec((1, 1, PAST_LEN, LANE), lambda b, hd, t: (l, b, 0, hd)),
                  pl.BlockSpec((tq, LANE),
                               lambda b, hd, t: (z_tile0 + b * tiles_per_seq + t, ZB_GP_MLA * Z_BLK // LANE + hd))],
        out_specs=pl.BlockSpec((tq, LANE), lambda b, hd, t: (b * tiles_per_seq + t, hd)),
        out_shape=jax.ShapeDtypeStruct((N_S, MLA_W), BF16),
        compiler_params=_params(3),
        name="attn_sample",
    )(q, k, v, kc, vc, z)


def _sgu_kernel(u_ref, v_ref, gp_ref, g_ref, w_ref, b_ref, o_ref, *, chunks):
    for c in range(chunks):
        rows = slice(c * SGU_CHUNK, (c + 1) * SGU_CHUNK)
        vn = _rms(v_ref[rows, :], g_ref[...]).astype(BF16)
        for g in range(SGU_GROUPS):
            cols = slice(g * LANE, (g + 1) * LANE)
            s = jnp.dot(w_ref[0, g].astype(BF16), vn[:, cols], preferred_element_type=F32) + b_ref[0, :, cols]
            o_ref[rows, cols] = (gp_ref[rows, cols] * (u_ref[rows, cols] * s)).astype(BF16)


def _sgu(z, g_sgu_l, w_sgu, b_sgu_exp, l):
    tm = 512
    return pl.pallas_call(
        functools.partial(_sgu_kernel, chunks=tm // SGU_CHUNK),
        grid=(N_TOK // tm,),
        in_specs=[pl.BlockSpec((tm, Z_BLK), lambda i: (i, ZB_U)),
                  pl.BlockSpec((tm, Z_BLK), lambda i: (i, ZB_VS)),
                  pl.BlockSpec((tm, Z_BLK), lambda i: (i, ZB_GP_SGU)),
                  pl.BlockSpec((1, SGU_W), lambda i: (0, 0)),
                  pl.BlockSpec((1, SGU_GROUPS, SGU_CHUNK, SGU_CHUNK), lambda i: (l, 0, 0, 0)),
                  pl.BlockSpec((1, SGU_CHUNK, SGU_W), lambda i: (l, 0, 0))],
        out_specs=pl.BlockSpec((tm, SGU_W), lambda i: (i, 0)),
        out_shape=jax.ShapeDtypeStruct((N_TOK, SGU_W), BF16),
        compiler_params=_params(1),
        name="sgu",
    )(z, z, z, g_sgu_l, w_sgu, b_sgu_exp)


def _log_sigmoid(x):
    return -(jnp.maximum(-x, 0.0) + jnp.log1p(jnp.exp(-jnp.abs(x))))


def _ret_kernel(q_ref, k_ref, v_ref, gp_ref, dec_ref, g_ref, *rest, n_chunks, has_state, emit_state):
    rest = list(rest)
    r0_ref = rest.pop(0) if has_state else None
    o_ref = rest.pop(0)
    rout_ref = rest.pop(0) if emit_state else None
    of_ref, ob_ref = rest
    C = RET_CHUNK
    shape = (C, C)
    ii = lax.broadcasted_iota(jnp.int32, shape, 0).astype(F32)
    jj = lax.broadcasted_iota(jnp.int32, shape, 1).astype(F32)
    lg_f = jnp.tile(_log_sigmoid(dec_ref[0, 0, 0]), (C // 8, 1))
    lg_b = jnp.tile(_log_sigmoid(dec_ref[0, 1, 0]), (C // 8, 1))
    d_f = ii - jj
    m_f = d_f >= 0
    dec_f = jnp.where(m_f, jnp.exp(jnp.where(m_f, d_f, 0.0) * lg_f), 0.0)
    xi_f = jnp.exp((ii + 1.0) * lg_f)
    zeta_f = jnp.exp((C - 1.0 - ii) * lg_f)
    gc_f = jnp.exp(C * lg_f)
    d_b = jj - ii
    m_b = d_b > 0
    dec_b = jnp.where(m_b, jnp.exp(jnp.where(m_b, d_b, 0.0) * lg_b), 0.0)
    xi_b = jnp.exp(((C - 1.0 - ii) + 1.0) * lg_b)
    zeta_b = jnp.exp((C - 1.0 - (C - 1.0 - ii)) * lg_b)
    gc_b = jnp.exp(C * lg_b)
    k_scale = RET_DK ** -0.5

    def chunk(c, r, dec, xi, zeta, gc, dst_ref):
        rows = pl.ds(pl.multiple_of(c * C, C), C)
        qc = q_ref[rows, :].astype(BF16)
        kc = k_ref[rows, :] * k_scale
        vc = v_ref[rows, :].astype(BF16)
        att = lax.dot_general(qc, kc.astype(BF16), _NT, preferred_element_type=F32) * dec
        o = jnp.dot(att.astype(BF16), vc, preferred_element_type=F32)
        o = o + jnp.dot(qc, r.astype(BF16), preferred_element_type=F32) * xi
        dst_ref[rows, :] = o
        kz_t = jnp.transpose(kc * zeta).astype(BF16)
        return gc * r + jnp.dot(kz_t, vc, preferred_element_type=F32)

    if has_state:
        r_f0 = r0_ref[0, 0, 0, 0]
        r_b0 = r0_ref[0, 0, 1, 0]
    else:
        r_f0 = jnp.zeros(shape, F32)
        r_b0 = jnp.zeros(shape, F32)

    def step(t, carry):
        r_f, r_b = carry
        r_f = chunk(t, r_f, dec_f, xi_f, zeta_f, gc_f, of_ref)
        r_b = chunk(n_chunks - 1 - t, r_b, dec_b, xi_b, zeta_b, gc_b, ob_ref)
        return r_f, r_b

    r_f, r_b = lax.fori_loop(0, n_chunks, step, (r_f0, r_b0))
    if emit_state:
        rout_ref[0, 0, 0, 0] = r_f
        rout_ref[0, 0, 1, 0] = r_b

    def finish(c, _):
        rows = pl.ds(pl.multiple_of(c * C, C), C)
        o = of_ref[rows, :] + ob_ref[rows, :]
        mu = jnp.mean(o, axis=-1, keepdims=True)
        var = jnp.mean(jnp.square(o - mu), axis=-1, keepdims=True)
        y = (o - mu) * lax.rsqrt(var + EPS) * g_ref[...]
        o_ref[rows, :] = (gp_ref[rows, :] * y).astype(BF16)
        return 0

    lax.fori_loop(0, n_chunks, finish, 0)


def _retention(z, decay_b, g_ret_l, state_ret, l, group):
    seq = SEQ if group == 0 else DEC_SEQ
    nb = BATCH if group == 0 else DEC_BATCH
    b0 = 0 if group == 0 else N_P // seq
    has_state = group == 1
    emit_state = group == 0

    def zspec(zb):
        return pl.BlockSpec((seq, LANE), lambda b, hd: (b0 + b, zb * Z_BLK // LANE + hd))

    in_specs = [zspec(ZB_RQ), zspec(ZB_RK), zspec(ZB_RV), zspec(ZB_GP_RET),
                pl.BlockSpec((1, 2, 1, 8, LANE), lambda b, hd: (l, 0, hd, 0, 0)),
                pl.BlockSpec((1, LANE), lambda b, hd: (0, hd))]
    args = [z, z, z, z, decay_b, g_ret_l]
    state_block = (1, 1, 2, 1, RET_DK, RET_DV)
    if has_state:
        in_specs.append(pl.BlockSpec(state_block, lambda b, hd: (b, l, 0, hd, 0, 0)))
        args.append(state_ret)
    out_specs = [pl.BlockSpec((seq, LANE), lambda b, hd: (b, hd))]
    out_shape = [jax.ShapeDtypeStruct((nb * seq, RET_W), BF16)]
    if emit_state:
        out_specs.append(pl.BlockSpec(state_block, lambda b, hd: (b, 0, 0, hd, 0, 0)))
        out_shape.append(jax.ShapeDtypeStruct((nb, 1, 2, RET_HEADS, RET_DK, RET_DV), F32))
    return pl.pallas_call(
        functools.partial(_ret_kernel, n_chunks=seq // RET_CHUNK, has_state=has_state, emit_state=emit_state),
        grid=(nb, RET_HEADS),
        in_specs=in_specs,
        out_specs=out_specs,
        out_shape=out_shape,
        scratch_shapes=[pltpu.VMEM((seq, RET_DV), F32), pltpu.VMEM((seq, RET_DV), F32)],
        compiler_params=_params(2),
        name="ret_sample" if has_state else "ret_prompt",
    )(*args)


def _merge_kernel(ap_m, as_m, a_sgu, ap_r, as_r, wm_ref, ws_ref, wr_ref, mm_ref, ms_ref, mr_ref, o_ref,
                  wm_bf, ws_bf, wr_bf, *, tiles_p):
    i = pl.program_id(1)

    @pl.when(i == 0)
    def _():
        _cast_rows(wm_bf, lambda r, s: wm_ref[0, r:r + s, :], MLA_W)
        _cast_rows(ws_bf, lambda r, s: ws_ref[0, r:r + s, :], SGU_W)
        _cast_rows(wr_bf, lambda r, s: wr_ref[0, r:r + s, :], RET_W)

    def run(a_mla_ref, a_ret_ref):
        y = mm_ref[...] * jnp.dot(a_mla_ref[...], wm_bf[...], preferred_element_type=F32)
        y = y + ms_ref[...] * jnp.dot(a_sgu[...], ws_bf[...], preferred_element_type=F32)
        y = y + mr_ref[...] * jnp.dot(a_ret_ref[...], wr_bf[...], preferred_element_type=F32)
        o_ref[...] = y.astype(BF16)

    @pl.when(i < tiles_p)
    def _():
        run(ap_m, ap_r)

    @pl.when(i >= tiles_p)
    def _():
        run(as_m, as_r)


def _merge(a_mla_p, a_mla_s, a_sgu, a_ret_p, a_ret_s, z, w_br_mla, w_br_sgu, w_br_ret, l):
    tm, tn = 512, 512
    tiles_p = N_P // tm
    nblk = D_MODEL // tn

    def p_spec():
        return pl.BlockSpec((tm, 1024), lambda n, i: (jnp.minimum(i, tiles_p - 1), 0))

    def s_spec():
        return pl.BlockSpec((tm, 1024), lambda n, i: (jnp.maximum(i - tiles_p, 0), 0))

    def w_spec():
        return pl.BlockSpec((1, 1024, tn), lambda n, i: (l, 0, n))

    def m_spec(k):
        return pl.BlockSpec((tm, tn), lambda n, i: (i, (ZB_MERGE * Z_BLK + k * D_MODEL) // tn + n))

    return pl.pallas_call(
        functools.partial(_merge_kernel, tiles_p=tiles_p),
        grid=(nblk, N_TOK // tm),
        in_specs=[p_spec(), s_spec(), pl.BlockSpec((tm, 1024), lambda n, i: (i, 0)), p_spec(), s_spec(),
                  w_spec(), w_spec(), w_spec(), m_spec(0), m_spec(1), m_spec(2)],
        out_specs=pl.BlockSpec((tm, tn), lambda n, i: (i, n)),
        out_shape=jax.ShapeDtypeStruct((N_TOK, D_MODEL), BF16),
        scratch_shapes=[pltpu.VMEM((1024, tn), BF16)] * 3,
        compiler_params=_params(2),
        name="merge",
    )(a_mla_p, a_mla_s, a_sgu, a_ret_p, a_ret_s, w_br_mla, w_br_sgu, w_br_ret, z, z, z)


def _out_kernel(y_ref, w_ref, x_ref, g_ref, mod_ref, o_ref, w_bf):
    @pl.when(pl.program_id(0) == 0)
    def _():
        _cast_rows(w_bf, lambda r, s: w_ref[0, r:r + s, :], D_MODEL)

    y = jnp.dot(y_ref[...], w_bf[...], preferred_element_type=F32)
    o_ref[...] = x_ref[...] + mod_ref[0, 2:3, :] * _rms(y, g_ref[...])


def _outproj(ymix, w_out, x, g_post_l, mod3, l):
    tm = 256
    return pl.pallas_call(
        _out_kernel,
        grid=(N_TOK // tm,),
        in_specs=[pl.BlockSpec((tm, D_MODEL), lambda i: (i, 0)),
                  pl.BlockSpec((1, D_MODEL, D_MODEL), lambda i: (l, 0, 0), pipeline_mode=pl.Buffered(1)),
                  pl.BlockSpec((tm, D_MODEL), lambda i: (i, 0)),
                  pl.BlockSpec((1, D_MODEL), lambda i: (0, 0)),
                  pl.BlockSpec((1, 3, D_MODEL), lambda i: (l * COND_ROWS + _cond_row(i, tm), 0, 0))],
        out_specs=pl.BlockSpec((tm, D_MODEL), lambda i: (i, 0)),
        out_shape=jax.ShapeDtypeStruct((N_TOK, D_MODEL), F32),
        scratch_shapes=[pltpu.VMEM((D_MODEL, D_MODEL), BF16)],
        compiler_params=_params(1),
        name="outproj",
    )(ymix, w_out, x, g_post_l, mod3)


def _rope_tables():
    pos = jnp.arange(DEC_SEQ)
    row = (pos // GRID_W).astype(F32)
    col = (pos % GRID_W).astype(F32)
    n_freq = QK_ROPE // 4
    inv = ROPE_BASE ** (-jnp.arange(n_freq, dtype=F32) / n_freq)
    ang_r = row[:, None] * inv
    ang_c = col[:, None] * inv
    zeros = jnp.zeros((DEC_SEQ, n_freq), F32)
    pad = jnp.zeros((DEC_SEQ, LANE - QK_ROPE), F32)
    cr, sr, cc, sc = jnp.cos(ang_r), jnp.sin(ang_r), jnp.cos(ang_c), jnp.sin(ang_c)
    c = jnp.concatenate([cr, cr, cc, cc, pad], axis=1)
    s1 = jnp.concatenate([-sr, zeros, -sc, zeros, pad], axis=1)
    s2 = jnp.concatenate([zeros, sr, zeros, sc, pad], axis=1)
    return c, s1, s2


def kernel(x_prompt, x_sample, cache_ckv, cache_krope, state_ret, c, c_ctx, w_mod, b_mod, g_pre, g_post, w_in,
           g_q, g_kv, w_uq, w_ukv, g_sgu, w_sgu, b_sgu, ret_decay, g_ret, w_br_mla, w_br_sgu, w_br_ret, w_out):
    cond8 = jnp.concatenate([c_ctx[None, :], c, jnp.zeros((COND_ROWS - 1 - DEC_BATCH, D_MODEL), F32)], axis=0)
    mod3 = _modulation(cond8, w_mod, b_mod).reshape(DEPTH * COND_ROWS, 3, D_MODEL)
    w_uq_pad = jnp.pad(w_uq.reshape(DEPTH, Q_LORA, MLA_HEADS, QK_NOPE + QK_ROPE),
                       ((0, 0), (0, 0), (0, 0), (0, HEAD_SLOT - QK_NOPE - QK_ROPE))
                       ).reshape(DEPTH, Q_LORA, MLA_HEADS * HEAD_SLOT)
    b_sgu_exp = jnp.repeat(b_sgu, LANE, axis=2)
    decay_b = jnp.broadcast_to(ret_decay[:, :, :, None, None], (DEPTH, 2, RET_HEADS, 8, LANE))
    rope_tabs = _rope_tables()
    kc, vc = _ctxkv(cache_ckv, cache_krope, w_ukv)

    x = jnp.concatenate([x_prompt.reshape(N_P, D_MODEL), x_sample.reshape(N_S, D_MODEL)], axis=0)
    ckvs, kropes, rets = [], [], []
    for l in range(DEPTH):
        h = _prenorm(x, g_pre[l][None, :], mod3, l)
        z = _inproj(h, w_in, l)
        q_p, k_p, v_p, ckvn, kr = _qkv(z, h, w_in, w_uq_pad, w_ukv, g_q[l][None, :], g_kv[l][None, :], None, l, 0)
        q_s, k_s, v_s = _qkv(z, h, w_in, w_uq_pad, w_ukv, g_q[l][None, :], g_kv[l][None, :], rope_tabs, l, 1)
        a_mla_p = _attn_prompt(q_p, k_p, v_p, z)
        a_mla_s = _attn_sample(q_s, k_s, v_s, kc, vc, z, l)
        a_sgu = _sgu(z, g_sgu[l][None, :], w_sgu, b_sgu_exp, l)
        a_ret_p, r_new = _retention(z, decay_b, g_ret[l][None, :], None, l, 0)
        (a_ret_s,) = _retention(z, decay_b, g_ret[l][None, :], state_ret, l, 1)
        ymix = _merge(a_mla_p, a_mla_s, a_sgu, a_ret_p, a_ret_s, z, w_br_mla, w_br_sgu, w_br_ret, l)
        x = _outproj(ymix, w_out, x, g_post[l][None, :], mod3, l)
        ckvs.append(ckvn)
        kropes.append(kr)
        rets.append(r_new)
    y_prompt = x[:N_P].reshape(BATCH, SEQ, D_MODEL)
    y_sample = x[N_P:].reshape(DEC_BATCH, DEC_SEQ, D_MODEL)
    return (y_prompt, y_sample, jnp.concatenate(ckvs, axis=1), jnp.concatenate(kropes, axis=1),
            jnp.concatenate(rets, axis=1))
```

```python
import functools

import jax
import jax.numpy as jnp
from jax import lax
from jax.experimental import pallas as pl
from jax.experimental.pallas import tpu as pltpu

F32 = jnp.float32
BF16 = jnp.bfloat16

D_MODEL = 2048
BATCH = 16
SEQ = 256
DEPTH = 4
DEC_BATCH = 2
DEC_SEQ = 2048
PAST_LEN = 256
GRID_W = 64
EPS = 1e-6
ROPE_BASE = 10000.0

MLA_HEADS = 8
QK_NOPE = 128
QK_ROPE = 64
V_HEAD = 128
Q_LORA = 512
KV_LORA = 512
MLA_W = MLA_HEADS * V_HEAD
SGU_CHUNK = 128
SGU_GROUPS = 8
SGU_W = 1024
RET_HEADS = 8
RET_DK = 128
RET_DV = 128
RET_W = RET_HEADS * RET_DV
RET_CHUNK = 128
N_BRANCH = 3

LANE = 128
N_P = BATCH * SEQ
N_S = DEC_BATCH * DEC_SEQ
N_TOK = N_P + N_S
COND_ROWS = 8
HEAD_SLOT = 2 * LANE
KR_COL = Q_LORA + KV_LORA
Z_BLK = 1024
N_ZBLK = 15
ZB_GP_MLA, ZB_U, ZB_VS, ZB_GP_SGU, ZB_RQ, ZB_RK, ZB_RV, ZB_GP_RET, ZB_MERGE = 1, 2, 3, 4, 5, 6, 7, 8, 9
VMEM_LIMIT = 56 * 1024 * 1024
LOG2E = 1.4426950408889634

_NT = (((1,), (1,)), ((), ()))


def _params(n_axes, vmem=VMEM_LIMIT):
    return pltpu.CompilerParams(dimension_semantics=("arbitrary",) * n_axes, vmem_limit_bytes=vmem)


def _rms(x, g):
    return x * lax.rsqrt(jnp.mean(x * x, axis=-1, keepdims=True) + EPS) * g


def _silu(x):
    return x * jax.nn.sigmoid(x)


def _cond_row(tile, tm):
    return jnp.where(tile < N_P // tm, 0, 1 + (tile - N_P // tm) // (DEC_SEQ // tm))


def _mod_spec(l, tm):
    return pl.BlockSpec((1, 3, D_MODEL), lambda i: (l * COND_ROWS + _cond_row(i, tm), 0, 0))


def _prenorm_value(x, g, mod_ref):
    return (_rms(x, g) * (1.0 + mod_ref[0, 1:2, :]) + mod_ref[0, 0:1, :]).astype(BF16)


def _mod_kernel(cond_ref, w_ref, b_ref, o_ref):
    s = _silu(cond_ref[...]).astype(BF16)
    o_ref[0] = jnp.dot(s, w_ref[0].astype(BF16), preferred_element_type=F32) + b_ref[0]


def _modulation(cond8, w_mod, b_mod):
    tn = 1024
    return pl.pallas_call(
        _mod_kernel,
        grid=(DEPTH, 3 * D_MODEL // tn),
        in_specs=[pl.BlockSpec((COND_ROWS, D_MODEL), lambda l, n: (0, 0)),
                  pl.BlockSpec((1, D_MODEL, tn), lambda l, n: (l, 0, n)),
                  pl.BlockSpec((1, 1, tn), lambda l, n: (l, 0, n))],
        out_specs=pl.BlockSpec((1, COND_ROWS, tn), lambda l, n: (l, 0, n)),
        out_shape=jax.ShapeDtypeStruct((DEPTH, COND_ROWS, 3 * D_MODEL), F32),
        compiler_params=_params(2),
        name="modulation",
    )(cond8, w_mod, b_mod.reshape(DEPTH, 1, 3 * D_MODEL))


def _prenorm_kernel(xp_ref, xs_ref, g_ref, mod_ref, h_ref, *, tiles_p):
    x = jnp.where(pl.program_id(0) < tiles_p, xp_ref[...], xs_ref[...])
    h_ref[...] = _prenorm_value(x, g_ref[...], mod_ref)


def _prenorm(x_p, x_s, g_pre_l, mod3):
    tm = 512
    tiles_p = N_P // tm
    return pl.pallas_call(
        functools.partial(_prenorm_kernel, tiles_p=tiles_p),
        grid=(N_TOK // tm,),
        in_specs=[pl.BlockSpec((tm, D_MODEL), lambda i: (jnp.minimum(i, tiles_p - 1), 0)),
                  pl.BlockSpec((tm, D_MODEL), lambda i: (jnp.maximum(i - tiles_p, 0), 0)),
                  pl.BlockSpec((1, D_MODEL), lambda i: (0, 0)),
                  _mod_spec(0, tm)],
        out_specs=pl.BlockSpec((tm, D_MODEL), lambda i: (i, 0)),
        out_shape=jax.ShapeDtypeStruct((N_TOK, D_MODEL), BF16),
        compiler_params=_params(1),
        name="prenorm",
    )(x_p, x_s, g_pre_l, mod3)


def _inproj_kernel(h_ref, w_ref, o_ref, wbf_ref):
    j = pl.program_id(0)

    @pl.when(pl.program_id(1) == 0)
    def _():
        step = 128
        for r in range(0, Z_BLK, step):
            wbf_ref[r:r + step, :] = w_ref[0, r:r + step, :].astype(BF16)

    acc = lax.dot_general(h_ref[...], wbf_ref[...], _NT, preferred_element_type=F32)
    is_silu = jnp.logical_or(jnp.logical_or(j == ZB_GP_MLA, j == ZB_GP_SGU), j == ZB_GP_RET)
    sig = jax.nn.sigmoid(acc)
    o_ref[...] = jnp.where(is_silu, acc * sig, jnp.where(j >= ZB_MERGE, sig, acc))


def _inproj(h, w_in_t, l):
    tm = 1024
    return pl.pallas_call(
        _inproj_kernel,
        grid=(N_ZBLK, N_TOK // tm),
        in_specs=[pl.BlockSpec((tm, D_MODEL), lambda j, i: (i, 0)),
                  pl.BlockSpec((pl.Element(1), pl.Element(Z_BLK), pl.Element(D_MODEL)),
                               lambda j, i: (l, pl.multiple_of(Z_BLK * j + jnp.where(j > 0, QK_ROPE, 0), QK_ROPE),
                                             0))],
        out_specs=pl.BlockSpec((tm, Z_BLK), lambda j, i: (i, j)),
        out_shape=jax.ShapeDtypeStruct((N_TOK, N_ZBLK * Z_BLK), F32),
        scratch_shapes=[pltpu.VMEM((Z_BLK, D_MODEL), BF16)],
        compiler_params=_params(2),
        name="inproj",
    )(h, w_in_t)


def _rope(x, c, s1, s2):
    return x * c + pltpu.roll(x, LANE - QK_ROPE // 4, 1) * s1 + pltpu.roll(x, QK_ROPE // 4, 1) * s2


def _qkv_kernel(z_ref, h_ref, wkr_ref, wuq_ref, wukv_ref, gq_ref, gkv_ref, *rest, rope):
    if rope:
        c_ref, s1_ref, s2_ref, q_ref, k_ref, v_ref = rest
    else:
        q_ref, k_ref, v_ref, ckvn_ref, kr_ref = rest
    cq = z_ref[:, :Q_LORA]
    ckv = z_ref[:, Q_LORA:]
    q = jnp.dot(_rms(cq, gq_ref[...]).astype(BF16), wuq_ref[0], preferred_element_type=F32)
    ckvn = _rms(ckv, gkv_ref[...])
    kv = jnp.dot(ckvn.astype(BF16), wukv_ref[0], preferred_element_type=F32)
    kr = lax.dot_general(h_ref[...], wkr_ref[0], _NT, preferred_element_type=F32)
    if rope:
        c, s1, s2 = c_ref[...], s1_ref[...], s2_ref[...]
        kr = _rope(kr, c, s1, s2)
    else:
        ckvn_ref[0, 0] = ckvn
        kr_ref[0, 0] = kr[:, :QK_ROPE]
    kr_bf = kr.astype(BF16)
    for hd in range(MLA_HEADS):
        lo = hd * HEAD_SLOT
        q_ref[:, lo:lo + LANE] = q[:, lo:lo + LANE].astype(BF16)
        qr = q[:, lo + LANE:lo + HEAD_SLOT]
        if rope:
            qr = _rope(qr, c, s1, s2)
        q_ref[:, lo + LANE:lo + HEAD_SLOT] = qr.astype(BF16)
        k_ref[:, lo:lo + LANE] = kv[:, lo:lo + LANE].astype(BF16)
        k_ref[:, lo + LANE:lo + HEAD_SLOT] = kr_bf
        v_ref[:, hd * LANE:(hd + 1) * LANE] = kv[:, lo + LANE:lo + HEAD_SLOT].astype(BF16)


def _qkv(z, h, w_kr, w_uq_pad, w_ukv, g_q_l, g_kv_l, rope_tabs, l, group):
    tm = 256
    n_tok = N_P if group == 0 else N_S
    t0 = 0 if group == 0 else N_P // tm
    rope = group == 1
    in_specs = [pl.BlockSpec((tm, Z_BLK), lambda i: (i + t0, 0)),
                pl.BlockSpec((tm, D_MODEL), lambda i: (i + t0, 0)),
                pl.BlockSpec((1, LANE, D_MODEL), lambda i: (l, 0, 0)),
                pl.BlockSpec((1, Q_LORA, MLA_HEADS * HEAD_SLOT), lambda i: (l, 0, 0)),
                pl.BlockSpec((1, KV_LORA, MLA_HEADS * HEAD_SLOT), lambda i: (l, 0, 0)),
                pl.BlockSpec((1, Q_LORA), lambda i: (0, 0)),
                pl.BlockSpec((1, KV_LORA), lambda i: (0, 0))]
    args = [z, h, w_kr, w_uq_pad, w_ukv, g_q_l, g_kv_l]
    out_specs = [pl.BlockSpec((tm, MLA_HEADS * HEAD_SLOT), lambda i: (i, 0)),
                 pl.BlockSpec((tm, MLA_HEADS * HEAD_SLOT), lambda i: (i, 0)),
                 pl.BlockSpec((tm, MLA_W), lambda i: (i, 0))]
    out_shape = [jax.ShapeDtypeStruct((n_tok, MLA_HEADS * HEAD_SLOT), BF16),
                 jax.ShapeDtypeStruct((n_tok, MLA_HEADS * HEAD_SLOT), BF16),
                 jax.ShapeDtypeStruct((n_tok, MLA_W), BF16)]
    if rope:
        tiles_per_seq = DEC_SEQ // tm
        in_specs += [pl.BlockSpec((tm, LANE), lambda i: (i % tiles_per_seq, 0))] * 3
        args += list(rope_tabs)
    else:
        assert tm == SEQ
        out_specs += [pl.BlockSpec((1, 1, SEQ, KV_LORA), lambda i: (i, 0, 0, 0)),
                      pl.BlockSpec((1, 1, SEQ, QK_ROPE), lambda i: (i, 0, 0, 0))]
        out_shape += [jax.ShapeDtypeStruct((BATCH, 1, SEQ, KV_LORA), F32),
                      jax.ShapeDtypeStruct((BATCH, 1, SEQ, QK_ROPE), F32)]
    return pl.pallas_call(
        functools.partial(_qkv_kernel, rope=rope),
        grid=(n_tok // tm,),
        in_specs=in_specs,
        out_specs=out_specs,
        out_shape=out_shape,
        compiler_params=_params(1),
        name="qkv_sample" if rope else "qkv_prompt",
    )(*args)


def _ctxkv_kernel(ckv_ref, kr_ref, w_ref, k_ref, v_ref):
    kv = jnp.dot(ckv_ref[0, 0].astype(BF16), w_ref[0], preferred_element_type=F32)
    kr = jnp.concatenate([kr_ref[0, 0], jnp.zeros((PAST_LEN, LANE - QK_ROPE), F32)], axis=1).astype(BF16)
    for hd in range(MLA_HEADS):
        lo = hd * HEAD_SLOT
        k_ref[0, 0, :, lo:lo + LANE] = kv[:, lo:lo + LANE].astype(BF16)
        k_ref[0, 0, :, lo + LANE:lo + HEAD_SLOT] = kr
        v_ref[0, 0, :, hd * LANE:(hd + 1) * LANE] = kv[:, lo + LANE:lo + HEAD_SLOT].astype(BF16)


def _ctxkv(cache_ckv, cache_krope, w_ukv):
    return pl.pallas_call(
        _ctxkv_kernel,
        grid=(DEPTH, DEC_BATCH),
        in_specs=[pl.BlockSpec((1, 1, PAST_LEN, KV_LORA), lambda l, b: (b, l, 0, 0)),
                  pl.BlockSpec((1, 1, PAST_LEN, QK_ROPE), lambda l, b: (b, l, 0, 0)),
                  pl.BlockSpec((1, KV_LORA, MLA_HEADS * HEAD_SLOT), lambda l, b: (l, 0, 0))],
        out_specs=[pl.BlockSpec((1, 1, PAST_LEN, MLA_HEADS * HEAD_SLOT), lambda l, b: (l, b, 0, 0)),
                   pl.BlockSpec((1, 1, PAST_LEN, MLA_W), lambda l, b: (l, b, 0, 0))],
        out_shape=[jax.ShapeDtypeStruct((DEPTH, DEC_BATCH, PAST_LEN, MLA_HEADS * HEAD_SLOT), BF16),
                   jax.ShapeDtypeStruct((DEPTH, DEC_BATCH, PAST_LEN, MLA_W), BF16)],
        compiler_params=_params(2),
        name="ctxkv",
    )(cache_ckv, cache_krope, w_ukv)


_EXP2_SCALE = (QK_NOPE + QK_ROPE) ** -0.5 * LOG2E


def _attn_prompt_kernel(q_ref, k_ref, v_ref, gp_ref, o_ref):
    for hd in range(MLA_HEADS):
        lo = hd * HEAD_SLOT
        s = lax.dot_general(q_ref[:, lo:lo + HEAD_SLOT], k_ref[:, lo:lo + HEAD_SLOT], _NT,
                            preferred_element_type=F32)
        e = jnp.exp2((s - jnp.max(s, axis=-1, keepdims=True)) * _EXP2_SCALE)
        p = (e * (1.0 / jnp.sum(e, axis=-1, keepdims=True))).astype(BF16)
        o = jnp.dot(p, v_ref[:, hd * LANE:(hd + 1) * LANE], preferred_element_type=F32)
        o_ref[:, hd * LANE:(hd + 1) * LANE] = (gp_ref[:, hd * LANE:(hd + 1) * LANE] * o).astype(BF16)


def _attn_prompt(q, k, v, z):
    return pl.pallas_call(
        _attn_prompt_kernel,
        grid=(BATCH,),
        in_specs=[pl.BlockSpec((SEQ, MLA_HEADS * HEAD_SLOT), lambda b: (b, 0)),
                  pl.BlockSpec((SEQ, MLA_HEADS * HEAD_SLOT), lambda b: (b, 0)),
                  pl.BlockSpec((SEQ, MLA_W), lambda b: (b, 0)),
                  pl.BlockSpec((SEQ, Z_BLK), lambda b: (b, ZB_GP_MLA))],
        out_specs=pl.BlockSpec((SEQ, MLA_W), lambda b: (b, 0)),
        out_shape=jax.ShapeDtypeStruct((N_P, MLA_W), BF16),
        compiler_params=_params(1),
        name="attn_prompt",
    )(q, k, v, z)


def _attn_sample_kernel(q_ref, k_ref, v_ref, kc_ref, vc_ref, gp_ref, o_ref, *, kchunk):
    q = q_ref[...]
    chunks = [(kc_ref[0, 0], vc_ref[0, 0])]
    chunks += [(k_ref[r:r + kchunk, :], v_ref[r:r + kchunk, :]) for r in range(0, DEC_SEQ, kchunk)]
    m = l = acc = None
    for k, v in chunks:
        s = lax.dot_general(q, k, _NT, preferred_element_type=F32)
        m_chunk = jnp.max(s, axis=-1, keepdims=True)
        if m is None:
            m = m_chunk
            e = jnp.exp2((s - m) * _EXP2_SCALE)
            l = jnp.sum(e, axis=-1, keepdims=True)
            acc = jnp.dot(e.astype(BF16), v, preferred_element_type=F32)
        else:
            m_new = jnp.maximum(m, m_chunk)
            alpha = jnp.exp2((m - m_new) * _EXP2_SCALE)
            e = jnp.exp2((s - m_new) * _EXP2_SCALE)
            l = alpha * l + jnp.sum(e, axis=-1, keepdims=True)
            acc = alpha * acc + jnp.dot(e.astype(BF16), v, preferred_element_type=F32)
            m = m_new
    o_ref[...] = (gp_ref[...] * (acc * (1.0 / l))).astype(BF16)


def _attn_sample(q, k, v, kc, vc, z, l):
    tq = 512
    tiles_per_seq = DEC_SEQ // tq
    z_tile0 = N_P // tq
    return pl.pallas_call(
        functools.partial(_attn_sample_kernel, kchunk=PAST_LEN),
        grid=(DEC_BATCH, MLA_HEADS, tiles_per_seq),
        in_specs=[pl.BlockSpec((tq, HEAD_SLOT), lambda b, hd, t: (b * tiles_per_seq + t, hd)),
                  pl.BlockSpec((DEC_SEQ, HEAD_SLOT), lambda b, hd, t: (b, hd)),
                  pl.BlockSpec((DEC_SEQ, LANE), lambda b, hd, t: (b, hd)),
                  pl.BlockSpec((1, 1, PAST_LEN, HEAD_SLOT), lambda b, hd, t: (l, b, 0, hd)),
                  pl.BlockSpec((1, 1, PAST_LEN, LANE), lambda b, hd, t: (l, b, 0, hd)),
                  pl.BlockSpec((tq, LANE),
                               lambda b, hd, t: (z_tile0 + b * tiles_per_seq + t, ZB_GP_MLA * Z_BLK // LANE + hd))],
        out_specs=pl.BlockSpec((tq, LANE), lambda b, hd, t: (b * tiles_per_seq + t, hd)),
        out_shape=jax.ShapeDtypeStruct((N_S, MLA_W), BF16),
        compiler_params=_params(3),
        name="attn_sample",
    )(q, k, v, kc, vc, z)


def _sgu_kernel(u_ref, v_ref, gp_ref, g_ref, w_ref, b_ref, o_ref, *, chunks):
    for c in range(chunks):
        rows = slice(c * SGU_CHUNK, (c + 1) * SGU_CHUNK)
        vn = _rms(v_ref[rows, :], g_ref[...]).astype(BF16)
        for g in range(SGU_GROUPS):
            cols = slice(g * LANE, (g + 1) * LANE)
            s = jnp.dot(w_ref[0, g], vn[:, cols], preferred_element_type=F32) + b_ref[0, :, cols]
            o_ref[rows, cols] = (gp_ref[rows, cols] * (u_ref[rows, cols] * s)).astype(BF16)


def _sgu(z, g_sgu_l, w_sgu, b_sgu_exp, l):
    tm = 512
    return pl.pallas_call(
        functools.partial(_sgu_kernel, chunks=tm // SGU_CHUNK),
        grid=(N_TOK // tm,),
        in_specs=[pl.BlockSpec((tm, Z_BLK), lambda i: (i, ZB_U)),
                  pl.BlockSpec((tm, Z_BLK), lambda i: (i, ZB_VS)),
                  pl.BlockSpec((tm, Z_BLK), lambda i: (i, ZB_GP_SGU)),
                  pl.BlockSpec((1, SGU_W), lambda i: (0, 0)),
                  pl.BlockSpec((1, SGU_GROUPS, SGU_CHUNK, SGU_CHUNK), lambda i: (l, 0, 0, 0)),
                  pl.BlockSpec((1, SGU_CHUNK, SGU_W), lambda i: (l, 0, 0))],
        out_specs=pl.BlockSpec((tm, SGU_W), lambda i: (i, 0)),
        out_shape=jax.ShapeDtypeStruct((N_TOK, SGU_W), BF16),
        compiler_params=_params(1),
        name="sgu",
    )(z, z, z, g_sgu_l, w_sgu, b_sgu_exp)


def _log_sigmoid(x):
    return -(jnp.maximum(-x, 0.0) + jnp.log1p(jnp.exp(-jnp.abs(x))))


def _ret_kernel(q_ref, k_ref, v_ref, gp_ref, dec_ref, g_ref, *rest, n_chunks, heads, has_state, emit_state):
    rest = list(rest)
    r0_ref = rest.pop(0) if has_state else None
    o_ref = rest.pop(0)
    rout_ref = rest.pop(0) if emit_state else None
    kv_ref, rs_ref = rest
    C = RET_CHUNK
    shape = (C, C)
    ii = lax.broadcasted_iota(jnp.int32, shape, 0).astype(F32)
    jj = lax.broadcasted_iota(jnp.int32, shape, 1).astype(F32)
    lower = ii >= jj
    k_scale = RET_DK ** -0.5

    for hd in range(heads):
        cols = slice(hd * LANE, (hd + 1) * LANE)
        lg_f = jnp.tile(_log_sigmoid(dec_ref[0, 0, hd]), (C // 8, 1))
        lg_b = jnp.tile(_log_sigmoid(dec_ref[0, 1, hd]), (C // 8, 1))
        dec = jnp.where(lower, jnp.exp(jnp.where(lower, ii - jj, 0.0) * lg_f),
                        jnp.exp(jnp.where(lower, 0.0, jj - ii) * lg_b))
        xi_f = jnp.exp((ii + 1.0) * lg_f)
        xi_b = jnp.exp(((C - 1.0 - ii) + 1.0) * lg_b)
        zeta_f = jnp.exp((C - 1.0 - jj) * lg_f)
        zeta_b = jnp.exp((C - 1.0 - (C - 1.0 - jj)) * lg_b)
        gc_f = jnp.exp(C * lg_f)
        gc_b = jnp.exp(C * lg_b)

        def rows_of(c):
            return slice(c * C, (c + 1) * C)

        for c in range(n_chunks):
            k_t = jnp.transpose(k_ref[rows_of(c), cols] * k_scale)
            kz = jnp.concatenate([k_t * zeta_f, k_t * zeta_b], axis=0).astype(BF16)
            kv_ref[hd * n_chunks + c] = jnp.dot(kz, v_ref[rows_of(c), cols].astype(BF16),
                                                preferred_element_type=F32)

        r_f = r0_ref[0, 0, 0, hd] if has_state else jnp.zeros(shape, F32)
        for c in range(n_chunks):
            rs_ref[hd * n_chunks + c, :, :C] = r_f.astype(BF16)
            r_f = gc_f * r_f + kv_ref[hd * n_chunks + c, :C, :]
        r_b = r0_ref[0, 0, 1, hd] if has_state else jnp.zeros(shape, F32)
        for c in reversed(range(n_chunks)):
            rs_ref[hd * n_chunks + c, :, C:] = r_b.astype(BF16)
            r_b = gc_b * r_b + kv_ref[hd * n_chunks + c, C:, :]
        if emit_state:
            rout_ref[0, 0, 0, hd] = r_f
            rout_ref[0, 0, 1, hd] = r_b

        for c in range(n_chunks):
            q = q_ref[rows_of(c), cols].astype(BF16)
            k = (k_ref[rows_of(c), cols] * k_scale).astype(BF16)
            att = lax.dot_general(q, k, _NT, preferred_element_type=F32) * dec
            o = jnp.dot(att.astype(BF16), v_ref[rows_of(c), cols].astype(BF16), preferred_element_type=F32)
            use_f = has_state or c > 0
            use_b = has_state or c < n_chunks - 1
            if use_f and use_b:
                inter = jnp.dot(q, rs_ref[hd * n_chunks + c], preferred_element_type=F32)
                o = o + inter[:, :C] * xi_f + inter[:, C:] * xi_b
            elif use_f:
                o = o + jnp.dot(q, rs_ref[hd * n_chunks + c, :, :C], preferred_element_type=F32) * xi_f
            elif use_b:
                o = o + jnp.dot(q, rs_ref[hd * n_chunks + c, :, C:], preferred_element_type=F32) * xi_b
            mu = jnp.mean(o, axis=-1, keepdims=True)
            var = jnp.mean(jnp.square(o - mu), axis=-1, keepdims=True)
            y = (o - mu) * lax.rsqrt(var + EPS) * g_ref[:, cols]
            o_ref[rows_of(c), cols] = (gp_ref[rows_of(c), cols] * y).astype(BF16)


def _retention(z, decay_b, g_ret_l, state_ret, l, group):
    seq = SEQ if group == 0 else DEC_SEQ
    nb = BATCH if group == 0 else DEC_BATCH
    heads = RET_HEADS if group == 0 else 1
    hblk = RET_HEADS // heads
    b0 = 0 if group == 0 else N_P // seq
    has_state = group == 1
    emit_state = group == 0
    n_chunks = seq // RET_CHUNK
    width = heads * LANE

    def zspec(zb):
        return pl.BlockSpec((seq, width), lambda b, hb: (b0 + b, zb * Z_BLK // width + hb))

    in_specs = [zspec(ZB_RQ), zspec(ZB_RK), zspec(ZB_RV), zspec(ZB_GP_RET),
                pl.BlockSpec((1, 2, heads, 8, LANE), lambda b, hb: (l, 0, hb, 0, 0)),
                pl.BlockSpec((1, width), lambda b, hb: (0, hb))]
    args = [z, z, z, z, decay_b, g_ret_l]
    state_block = (1, 1, 2, heads, RET_DK, RET_DV)
    if has_state:
        in_specs.append(pl.BlockSpec(state_block, lambda b, hb: (b, l, 0, hb, 0, 0)))
        args.append(state_ret)
    out_specs = [pl.BlockSpec((seq, width), lambda b, hb: (b, hb))]
    out_shape = [jax.ShapeDtypeStruct((nb * seq, RET_W), BF16)]
    if emit_state:
        out_specs.append(pl.BlockSpec(state_block, lambda b, hb: (b, 0, 0, hb, 0, 0)))
        out_shape.append(jax.ShapeDtypeStruct((nb, 1, 2, RET_HEADS, RET_DK, RET_DV), F32))
    return pl.pallas_call(
        functools.partial(_ret_kernel, n_chunks=n_chunks, heads=heads, has_state=has_state,
                          emit_state=emit_state),
        grid=(nb, hblk),
        in_specs=in_specs,
        out_specs=out_specs,
        out_shape=out_shape,
        scratch_shapes=[pltpu.VMEM((heads * n_chunks, 2 * RET_CHUNK, RET_DV), F32),
                        pltpu.VMEM((heads * n_chunks, RET_CHUNK, 2 * RET_CHUNK), BF16)],
        compiler_params=_params(2),
        name="ret_sample" if has_state else "ret_prompt",
    )(*args)


def _tail_kernel(ap_m, as_m, a_sgu, ap_r, as_r, m0, m1, m2, m3, m4, m5, wm_ref, ws_ref, wr_ref, wo_ref,
                 gpost_ref, mod_ref, *rest, tiles_p, first, last):
    i = pl.program_id(0)
    is_p = i < tiles_p
    rest = list(rest)
    if first:
        x = jnp.where(is_p, rest.pop(0)[...], rest.pop(0)[...])
    else:
        x = rest.pop(0)[...]
    a_mla = jnp.where(is_p, ap_m[...], as_m[...])
    a_ret = jnp.where(is_p, ap_r[...], as_r[...])
    y_mla = jnp.dot(a_mla, wm_ref[0], preferred_element_type=F32)
    y = jnp.concatenate([m0[...], m1[...]], axis=1) * y_mla
    y_sgu = jnp.dot(a_sgu[...], ws_ref[0], preferred_element_type=F32)
    y = y + jnp.concatenate([m2[...], m3[...]], axis=1) * y_sgu
    y_ret = jnp.dot(a_ret, wr_ref[0], preferred_element_type=F32)
    y = y + jnp.concatenate([m4[...], m5[...]], axis=1) * y_ret
    out = jnp.dot(y.astype(BF16), wo_ref[0], preferred_element_type=F32)
    x_new = x + mod_ref[0, 2:3, :] * _rms(out, gpost_ref[...])
    if last:
        yp_ref, ys_ref = rest

        @pl.when(is_p)
        def _():
            yp_ref[...] = x_new

        @pl.when(jnp.logical_not(is_p))
        def _():
            ys_ref[...] = x_new
    else:
        gpre_ref, modn_ref, x_out, h_out = rest
        x_out[...] = x_new
        h_out[...] = _prenorm_value(x_new, gpre_ref[...], modn_ref)


def _tail(a_mla_p, a_mla_s, a_sgu, a_ret_p, a_ret_s, z, w_br_mla, w_br_sgu, w_br_ret, w_out, xs, g_post_l,
          g_pre_next, mod3, l):
    tm = 256
    tiles_p = N_P // tm
    last = g_pre_next is None
    first = len(xs) == 2

    def p_spec():
        return pl.BlockSpec((tm, 1024), lambda i: (jnp.minimum(i, tiles_p - 1), 0))

    def s_spec():
        return pl.BlockSpec((tm, 1024), lambda i: (jnp.maximum(i - tiles_p, 0), 0))

    def w_spec(rows):
        return pl.BlockSpec((1, rows, D_MODEL), lambda i: (l, 0, 0), pipeline_mode=pl.Buffered(1))

    def row_spec():
        return pl.BlockSpec((tm, D_MODEL), lambda i: (i, 0))

    in_specs = [p_spec(), s_spec(), pl.BlockSpec((tm, 1024), lambda i: (i, 0)), p_spec(), s_spec()]
    in_specs += [pl.BlockSpec((tm, Z_BLK), lambda i, k=k: (i, ZB_MERGE + k)) for k in range(2 * N_BRANCH)]
    in_specs += [w_spec(1024), w_spec(1024), w_spec(1024), w_spec(D_MODEL),
                 pl.BlockSpec((1, D_MODEL), lambda i: (0, 0)), _mod_spec(l, tm)]
    args = [a_mla_p, a_mla_s, a_sgu, a_ret_p, a_ret_s] + [z] * (2 * N_BRANCH)
    args += [w_br_mla, w_br_sgu, w_br_ret, w_out, g_post_l, mod3]
    if first:
        in_specs += [pl.BlockSpec((tm, D_MODEL), lambda i: (jnp.minimum(i, tiles_p - 1), 0)),
                     pl.BlockSpec((tm, D_MODEL), lambda i: (jnp.maximum(i - tiles_p, 0), 0))]
    else:
        in_specs += [row_spec()]
    args += list(xs)
    if last:
        out_specs = [pl.BlockSpec((tm, D_MODEL), lambda i: (jnp.minimum(i, tiles_p - 1), 0)),
                     pl.BlockSpec((tm, D_MODEL), lambda i: (jnp.maximum(i - tiles_p, 0), 0))]
        out_shape = [jax.ShapeDtypeStruct((N_P, D_MODEL), F32), jax.ShapeDtypeStruct((N_S, D_MODEL), F32)]
    else:
        in_specs += [pl.BlockSpec((1, D_MODEL), lambda i: (0, 0)), _mod_spec(l + 1, tm)]
        args += [g_pre_next, mod3]
        out_specs = [row_spec(), row_spec()]
        out_shape = [jax.ShapeDtypeStruct((N_TOK, D_MODEL), F32), jax.ShapeDtypeStruct((N_TOK, D_MODEL), BF16)]
    return pl.pallas_call(
        functools.partial(_tail_kernel, tiles_p=tiles_p, first=first, last=last),
        grid=(N_TOK // tm,),
        in_specs=in_specs,
        out_specs=out_specs,
        out_shape=out_shape,
        compiler_params=_params(1),
        name="tail_last" if last else "tail",
    )(*args)


def _rope_tables():
    pos = jnp.arange(DEC_SEQ)
    row = (pos // GRID_W).astype(F32)
    col = (pos % GRID_W).astype(F32)
    n_freq = QK_ROPE // 4
    inv = ROPE_BASE ** (-jnp.arange(n_freq, dtype=F32) / n_freq)
    ang_r = row[:, None] * inv
    ang_c = col[:, None] * inv
    zeros = jnp.zeros((DEC_SEQ, n_freq), F32)
    pad = jnp.zeros((DEC_SEQ, LANE - QK_ROPE), F32)
    cr, sr, cc, sc = jnp.cos(ang_r), jnp.sin(ang_r), jnp.cos(ang_c), jnp.sin(ang_c)
    c = jnp.concatenate([cr, cr, cc, cc, pad], axis=1)
    s1 = jnp.concatenate([-sr, zeros, -sc, zeros, pad], axis=1)
    s2 = jnp.concatenate([zeros, sr, zeros, sc, pad], axis=1)
    return c, s1, s2


def kernel(x_prompt, x_sample, cache_ckv, cache_krope, state_ret, c, c_ctx, w_mod, b_mod, g_pre, g_post, w_in,
           g_q, g_kv, w_uq, w_ukv, g_sgu, w_sgu, b_sgu, ret_decay, g_ret, w_br_mla, w_br_sgu, w_br_ret, w_out):
    cond8 = jnp.concatenate([c_ctx[None, :], c, jnp.zeros((COND_ROWS - 1 - DEC_BATCH, D_MODEL), F32)], axis=0)
    mod3 = _modulation(cond8, w_mod, b_mod).reshape(DEPTH * COND_ROWS, 3, D_MODEL)
    w_in_t = jnp.swapaxes(w_in, 1, 2)
    w_kr = jnp.pad(w_in_t[:, KR_COL:KR_COL + QK_ROPE, :], ((0, 0), (0, LANE - QK_ROPE), (0, 0))).astype(BF16)
    w_uq_pad = jnp.pad(w_uq.reshape(DEPTH, Q_LORA, MLA_HEADS, QK_NOPE + QK_ROPE),
                       ((0, 0), (0, 0), (0, 0), (0, HEAD_SLOT - QK_NOPE - QK_ROPE))
                       ).reshape(DEPTH, Q_LORA, MLA_HEADS * HEAD_SLOT).astype(BF16)
    w_ukv_bf = w_ukv.astype(BF16)
    w_sgu_bf = w_sgu.astype(BF16)
    w_br = [w.astype(BF16) for w in (w_br_mla, w_br_sgu, w_br_ret)]
    w_out_bf = w_out.astype(BF16)
    b_sgu_exp = jnp.repeat(b_sgu, LANE, axis=2)
    decay_b = jnp.broadcast_to(ret_decay[:, :, :, None, None], (DEPTH, 2, RET_HEADS, 8, LANE))
    rope_tabs = _rope_tables()
    kc, vc = _ctxkv(cache_ckv, cache_krope, w_ukv_bf)

    xs = (x_prompt.reshape(N_P, D_MODEL), x_sample.reshape(N_S, D_MODEL))
    h = _prenorm(*xs, g_pre[0][None, :], mod3)
    ckvs, kropes, rets = [], [], []
    for l in range(DEPTH):
        z = _inproj(h, w_in_t, l)
        q_p, k_p, v_p, ckvn, kr = _qkv(z, h, w_kr, w_uq_pad, w_ukv_bf, g_q[l][None, :], g_kv[l][None, :], None, l, 0)
        q_s, k_s, v_s = _qkv(z, h, w_kr, w_uq_pad, w_ukv_bf, g_q[l][None, :], g_kv[l][None, :], rope_tabs, l, 1)
        a_mla_p = _attn_prompt(q_p, k_p, v_p, z)
        a_mla_s = _attn_sample(q_s, k_s, v_s, kc, vc, z, l)
        a_sgu = _sgu(z, g_sgu[l][None, :], w_sgu_bf, b_sgu_exp, l)
        a_ret_p, r_new = _retention(z, decay_b, g_ret[l][None, :], None, l, 0)
        (a_ret_s,) = _retention(z, decay_b, g_ret[l][None, :], state_ret, l, 1)
        g_pre_next = g_pre[l + 1][None, :] if l + 1 < DEPTH else None
        outs = _tail(a_mla_p, a_mla_s, a_sgu, a_ret_p, a_ret_s, z, *w_br, w_out_bf, xs, g_post[l][None, :],
                     g_pre_next, mod3, l)
        if l + 1 < DEPTH:
            x, h = outs
            xs = (x,)
        else:
            y_p, y_s = outs
        ckvs.append(ckvn)
        kropes.append(kr)
        rets.append(r_new)
    return (y_p.reshape(BATCH, SEQ, D_MODEL), y_s.reshape(DEC_BATCH, DEC_SEQ, D_MODEL),
            jnp.concatenate(ckvs, axis=1), jnp.concatenate(kropes, axis=1), jnp.concatenate(rets, axis=1))
```

```python
import functools

import jax
import jax.numpy as jnp
from jax import lax
from jax.experimental import pallas as pl
from jax.experimental.pallas import tpu as pltpu

F32 = jnp.float32
BF16 = jnp.bfloat16

D_MODEL = 2048
BATCH = 16
SEQ = 256
DEPTH = 4
DEC_BATCH = 2
DEC_SEQ = 2048
PAST_LEN = 256
GRID_W = 64
EPS = 1e-6
ROPE_BASE = 10000.0

MLA_HEADS = 8
QK_NOPE = 128
QK_ROPE = 64
V_HEAD = 128
Q_LORA = 512
KV_LORA = 512
MLA_W = MLA_HEADS * V_HEAD
SGU_CHUNK = 128
SGU_GROUPS = 8
SGU_W = 1024
RET_HEADS = 8
RET_DK = 128
RET_DV = 128
RET_W = RET_HEADS * RET_DV
RET_CHUNK = 128
N_BRANCH = 3

LANE = 128
N_P = BATCH * SEQ
N_S = DEC_BATCH * DEC_SEQ
N_TOK = N_P + N_S
COND_ROWS = 8
HEAD_SLOT = 2 * LANE
KR_COL = Q_LORA + KV_LORA
Z_BLK = 1024
N_ZBLK = 15
ZB_GP_MLA, ZB_U, ZB_VS, ZB_GP_SGU, ZB_RQ, ZB_RK, ZB_RV, ZB_GP_RET, ZB_MERGE = 1, 2, 3, 4, 5, 6, 7, 8, 9
VMEM_LIMIT = 56 * 1024 * 1024
LOG2E = 1.4426950408889634

_NT = (((1,), (1,)), ((), ()))


def _params(n_axes, vmem=VMEM_LIMIT):
    return pltpu.CompilerParams(dimension_semantics=("arbitrary",) * n_axes, vmem_limit_bytes=vmem)


def _rms(x, g):
    return x * lax.rsqrt(jnp.mean(x * x, axis=-1, keepdims=True) + EPS) * g


def _silu(x):
    return x * jax.nn.sigmoid(x)


def _cond_row(tile, tm):
    return jnp.where(tile < N_P // tm, 0, 1 + (tile - N_P // tm) // (DEC_SEQ // tm))


def _mod_spec(l, tm):
    return pl.BlockSpec((1, 3, D_MODEL), lambda i: (l * COND_ROWS + _cond_row(i, tm), 0, 0))


def _prenorm_value(x, g, mod_ref):
    return (_rms(x, g) * (1.0 + mod_ref[0, 1:2, :]) + mod_ref[0, 0:1, :]).astype(BF16)


def _mod_kernel(cond_ref, w_ref, b_ref, o_ref):
    s = _silu(cond_ref[...]).astype(BF16)
    o_ref[0] = jnp.dot(s, w_ref[0].astype(BF16), preferred_element_type=F32) + b_ref[0]


def _modulation(cond8, w_mod, b_mod):
    tn = 1024
    return pl.pallas_call(
        _mod_kernel,
        grid=(DEPTH, 3 * D_MODEL // tn),
        in_specs=[pl.BlockSpec((COND_ROWS, D_MODEL), lambda l, n: (0, 0)),
                  pl.BlockSpec((1, D_MODEL, tn), lambda l, n: (l, 0, n)),
                  pl.BlockSpec((1, 1, tn), lambda l, n: (l, 0, n))],
        out_specs=pl.BlockSpec((1, COND_ROWS, tn), lambda l, n: (l, 0, n)),
        out_shape=jax.ShapeDtypeStruct((DEPTH, COND_ROWS, 3 * D_MODEL), F32),
        compiler_params=_params(2),
        name="modulation",
    )(cond8, w_mod, b_mod.reshape(DEPTH, 1, 3 * D_MODEL))


def _prenorm_kernel(xp_ref, xs_ref, g_ref, mod_ref, h_ref, *, tiles_p):
    x = jnp.where(pl.program_id(0) < tiles_p, xp_ref[...], xs_ref[...])
    h_ref[...] = _prenorm_value(x, g_ref[...], mod_ref)


def _prenorm(x_p, x_s, g_pre_l, mod3):
    tm = 512
    tiles_p = N_P // tm
    return pl.pallas_call(
        functools.partial(_prenorm_kernel, tiles_p=tiles_p),
        grid=(N_TOK // tm,),
        in_specs=[pl.BlockSpec((tm, D_MODEL), lambda i: (jnp.minimum(i, tiles_p - 1), 0)),
                  pl.BlockSpec((tm, D_MODEL), lambda i: (jnp.maximum(i - tiles_p, 0), 0)),
                  pl.BlockSpec((1, D_MODEL), lambda i: (0, 0)),
                  _mod_spec(0, tm)],
        out_specs=pl.BlockSpec((tm, D_MODEL), lambda i: (i, 0)),
        out_shape=jax.ShapeDtypeStruct((N_TOK, D_MODEL), BF16),
        compiler_params=_params(1),
        name="prenorm",
    )(x_p, x_s, g_pre_l, mod3)


def _inproj_kernel(h_ref, w_ref, o_ref, wbf_ref):
    j = pl.program_id(0)

    @pl.when(pl.program_id(1) == 0)
    def _():
        step = 128
        for r in range(0, Z_BLK, step):
            wbf_ref[r:r + step, :] = w_ref[0, r:r + step, :].astype(BF16)

    acc = lax.dot_general(h_ref[...], wbf_ref[...], _NT, preferred_element_type=F32)
    is_silu = jnp.logical_or(jnp.logical_or(j == ZB_GP_MLA, j == ZB_GP_SGU), j == ZB_GP_RET)
    sig = 0.5 * jnp.tanh(0.5 * acc) + 0.5
    o_ref[...] = jnp.where(is_silu, acc * sig, jnp.where(j >= ZB_MERGE, sig, acc))


def _inproj(h, w_in_t, l):
    tm = 1024
    return pl.pallas_call(
        _inproj_kernel,
        grid=(N_ZBLK, N_TOK // tm),
        in_specs=[pl.BlockSpec((tm, D_MODEL), lambda j, i: (i, 0)),
                  pl.BlockSpec((pl.Element(1), pl.Element(Z_BLK), pl.Element(D_MODEL)),
                               lambda j, i: (l, pl.multiple_of(Z_BLK * j + jnp.where(j > 0, QK_ROPE, 0), QK_ROPE),
                                             0))],
        out_specs=pl.BlockSpec((tm, Z_BLK), lambda j, i: (i, j)),
        out_shape=jax.ShapeDtypeStruct((N_TOK, N_ZBLK * Z_BLK), F32),
        scratch_shapes=[pltpu.VMEM((Z_BLK, D_MODEL), BF16)],
        compiler_params=_params(2),
        name="inproj",
    )(h, w_in_t)


def _rope(x, c, s1, s2):
    return x * c + pltpu.roll(x, LANE - QK_ROPE // 4, 1) * s1 + pltpu.roll(x, QK_ROPE // 4, 1) * s2


def _qkv_kernel(z_ref, h_ref, wkr_ref, wuq_ref, wukv_ref, gq_ref, gkv_ref, *rest, rope):
    if rope:
        c_ref, s1_ref, s2_ref, q_ref, k_ref, v_ref = rest
    else:
        q_ref, k_ref, v_ref, ckvn_ref, kr_ref = rest[-5:]
    cq = z_ref[:, :Q_LORA]
    ckv = z_ref[:, Q_LORA:]
    q = jnp.dot(_rms(cq, gq_ref[...]).astype(BF16), wuq_ref[0], preferred_element_type=F32)
    ckvn = _rms(ckv, gkv_ref[...])
    kv = jnp.dot(ckvn.astype(BF16), wukv_ref[0], preferred_element_type=F32)
    kr = lax.dot_general(h_ref[...], wkr_ref[0], _NT, preferred_element_type=F32)
    if rope:
        c, s1, s2 = c_ref[...], s1_ref[...], s2_ref[...]
        kr = _rope(kr, c, s1, s2)
    else:
        for s in range(ckvn_ref.shape[1]):
            ckvn_ref[0, s] = ckvn if s == 0 else jnp.zeros_like(ckvn)
            kr_ref[0, s] = kr[:, :QK_ROPE] if s == 0 else jnp.zeros((kr.shape[0], QK_ROPE), F32)
    kr_bf = kr.astype(BF16)
    for hd in range(MLA_HEADS):
        lo = hd * HEAD_SLOT
        q_ref[:, lo:lo + LANE] = q[:, lo:lo + LANE].astype(BF16)
        qr = q[:, lo + LANE:lo + HEAD_SLOT]
        if rope:
            qr = _rope(qr, c, s1, s2)
        q_ref[:, lo + LANE:lo + HEAD_SLOT] = qr.astype(BF16)
        k_ref[:, lo:lo + LANE] = kv[:, lo:lo + LANE].astype(BF16)
        k_ref[:, lo + LANE:lo + HEAD_SLOT] = kr_bf
        v_ref[:, hd * LANE:(hd + 1) * LANE] = kv[:, lo + LANE:lo + HEAD_SLOT].astype(BF16)


def _qkv(z, h, w_kr, w_uq_pad, w_ukv, g_q_l, g_kv_l, rope_tabs, l, group, stacks=None):
    tm = 256
    aliases = {}
    n_tok = N_P if group == 0 else N_S
    t0 = 0 if group == 0 else N_P // tm
    rope = group == 1
    in_specs = [pl.BlockSpec((tm, Z_BLK), lambda i: (i + t0, 0)),
                pl.BlockSpec((tm, D_MODEL), lambda i: (i + t0, 0)),
                pl.BlockSpec((1, LANE, D_MODEL), lambda i: (l, 0, 0)),
                pl.BlockSpec((1, Q_LORA, MLA_HEADS * HEAD_SLOT), lambda i: (l, 0, 0)),
                pl.BlockSpec((1, KV_LORA, MLA_HEADS * HEAD_SLOT), lambda i: (l, 0, 0)),
                pl.BlockSpec((1, Q_LORA), lambda i: (0, 0)),
                pl.BlockSpec((1, KV_LORA), lambda i: (0, 0))]
    args = [z, h, w_kr, w_uq_pad, w_ukv, g_q_l, g_kv_l]
    out_specs = [pl.BlockSpec((tm, MLA_HEADS * HEAD_SLOT), lambda i: (i, 0)),
                 pl.BlockSpec((tm, MLA_HEADS * HEAD_SLOT), lambda i: (i, 0)),
                 pl.BlockSpec((tm, MLA_W), lambda i: (i, 0))]
    out_shape = [jax.ShapeDtypeStruct((n_tok, MLA_HEADS * HEAD_SLOT), BF16),
                 jax.ShapeDtypeStruct((n_tok, MLA_HEADS * HEAD_SLOT), BF16),
                 jax.ShapeDtypeStruct((n_tok, MLA_W), BF16)]
    if rope:
        tiles_per_seq = DEC_SEQ // tm
        in_specs += [pl.BlockSpec((tm, LANE), lambda i: (i % tiles_per_seq, 0))] * 3
        args += list(rope_tabs)
    else:
        assert tm == SEQ
        assert (l == 0) == (stacks is None)
        slices = DEPTH if stacks is None else 1
        out_specs += [pl.BlockSpec((1, slices, SEQ, KV_LORA), lambda i: (i, l, 0, 0)),
                      pl.BlockSpec((1, slices, SEQ, QK_ROPE), lambda i: (i, l, 0, 0))]
        out_shape += [jax.ShapeDtypeStruct((BATCH, DEPTH, SEQ, KV_LORA), F32),
                      jax.ShapeDtypeStruct((BATCH, DEPTH, SEQ, QK_ROPE), F32)]
        if stacks is not None:
            aliases = {len(args): 3, len(args) + 1: 4}
            in_specs += [pl.BlockSpec(memory_space=pl.ANY)] * 2
            args += list(stacks)
    return pl.pallas_call(
        functools.partial(_qkv_kernel, rope=rope),
        grid=(n_tok // tm,),
        in_specs=in_specs,
        out_specs=out_specs,
        out_shape=out_shape,
        input_output_aliases=aliases,
        compiler_params=_params(1),
        name="qkv_sample" if rope else "qkv_prompt",
    )(*args)


def _ctxkv_kernel(ckv_ref, kr_ref, w_ref, k_ref, v_ref):
    kv = jnp.dot(ckv_ref[0, 0].astype(BF16), w_ref[0], preferred_element_type=F32)
    kr = jnp.concatenate([kr_ref[0, 0], jnp.zeros((PAST_LEN, LANE - QK_ROPE), F32)], axis=1).astype(BF16)
    for hd in range(MLA_HEADS):
        lo = hd * HEAD_SLOT
        k_ref[0, 0, :, lo:lo + LANE] = kv[:, lo:lo + LANE].astype(BF16)
        k_ref[0, 0, :, lo + LANE:lo + HEAD_SLOT] = kr
        v_ref[0, 0, :, hd * LANE:(hd + 1) * LANE] = kv[:, lo + LANE:lo + HEAD_SLOT].astype(BF16)


def _ctxkv(cache_ckv, cache_krope, w_ukv):
    return pl.pallas_call(
        _ctxkv_kernel,
        grid=(DEPTH, DEC_BATCH),
        in_specs=[pl.BlockSpec((1, 1, PAST_LEN, KV_LORA), lambda l, b: (b, l, 0, 0)),
                  pl.BlockSpec((1, 1, PAST_LEN, QK_ROPE), lambda l, b: (b, l, 0, 0)),
                  pl.BlockSpec((1, KV_LORA, MLA_HEADS * HEAD_SLOT), lambda l, b: (l, 0, 0))],
        out_specs=[pl.BlockSpec((1, 1, PAST_LEN, MLA_HEADS * HEAD_SLOT), lambda l, b: (l, b, 0, 0)),
                   pl.BlockSpec((1, 1, PAST_LEN, MLA_W), lambda l, b: (l, b, 0, 0))],
        out_shape=[jax.ShapeDtypeStruct((DEPTH, DEC_BATCH, PAST_LEN, MLA_HEADS * HEAD_SLOT), BF16),
                   jax.ShapeDtypeStruct((DEPTH, DEC_BATCH, PAST_LEN, MLA_W), BF16)],
        compiler_params=_params(2),
        name="ctxkv",
    )(cache_ckv, cache_krope, w_ukv)


_EXP2_SCALE = (QK_NOPE + QK_ROPE) ** -0.5 * LOG2E


def _attn_prompt_kernel(q_ref, k_ref, v_ref, gp_ref, o_ref):
    for hd in range(MLA_HEADS):
        lo = hd * HEAD_SLOT
        s = lax.dot_general(q_ref[:, lo:lo + HEAD_SLOT], k_ref[:, lo:lo + HEAD_SLOT], _NT,
                            preferred_element_type=F32)
        e = jnp.exp2((s - jnp.max(s, axis=-1, keepdims=True)) * _EXP2_SCALE)
        p = (e * (1.0 / jnp.sum(e, axis=-1, keepdims=True))).astype(BF16)
        o = jnp.dot(p, v_ref[:, hd * LANE:(hd + 1) * LANE], preferred_element_type=F32)
        o_ref[:, hd * LANE:(hd + 1) * LANE] = (gp_ref[:, hd * LANE:(hd + 1) * LANE] * o).astype(BF16)


def _attn_prompt(q, k, v, z):
    return pl.pallas_call(
        _attn_prompt_kernel,
        grid=(BATCH,),
        in_specs=[pl.BlockSpec((SEQ, MLA_HEADS * HEAD_SLOT), lambda b: (b, 0)),
                  pl.BlockSpec((SEQ, MLA_HEADS * HEAD_SLOT), lambda b: (b, 0)),
                  pl.BlockSpec((SEQ, MLA_W), lambda b: (b, 0)),
                  pl.BlockSpec((SEQ, Z_BLK), lambda b: (b, ZB_GP_MLA))],
        out_specs=pl.BlockSpec((SEQ, MLA_W), lambda b: (b, 0)),
        out_shape=jax.ShapeDtypeStruct((N_P, MLA_W), BF16),
        compiler_params=_params(1),
        name="attn_prompt",
    )(q, k, v, z)


def _attn_sample_kernel(q_ref, k_ref, v_ref, kc_ref, vc_ref, gp_ref, o_ref, *, kchunk, heads):
    def lane_fold(x):
        out = x[:, :LANE]
        for c0 in range(LANE, kchunk, LANE):
            out = out + x[:, c0:c0 + LANE]
        return out

    for hd in range(heads):
        ks = slice(hd * HEAD_SLOT, (hd + 1) * HEAD_SLOT)
        vs = slice(hd * LANE, (hd + 1) * LANE)
        q = q_ref[:, ks]
        chunks = [(kc_ref[0, 0, :, ks], vc_ref[0, 0, :, vs])]
        chunks += [(k_ref[r:r + kchunk, ks], v_ref[r:r + kchunk, vs]) for r in range(0, DEC_SEQ, kchunk)]
        m = l = acc = None
        for k, v in chunks:
            s = lax.dot_general(q, k, _NT, preferred_element_type=F32)
            m_chunk = jnp.max(s, axis=-1, keepdims=True)
            if m is None:
                m = m_chunk
                e = jnp.exp2((s - m) * _EXP2_SCALE)
                l = lane_fold(e)
                acc = jnp.dot(e.astype(BF16), v, preferred_element_type=F32)
            else:
                m_new = jnp.maximum(m, m_chunk)
                alpha = jnp.exp2((m - m_new) * _EXP2_SCALE)
                e = jnp.exp2((s - m_new) * _EXP2_SCALE)
                l = alpha * l + lane_fold(e)
                acc = alpha * acc + jnp.dot(e.astype(BF16), v, preferred_element_type=F32)
                m = m_new
        inv = 1.0 / jnp.sum(l, axis=-1, keepdims=True)
        o_ref[:, vs] = (gp_ref[:, vs] * (acc * inv)).astype(BF16)


def _attn_sample(q, k, v, kc, vc, z, l):
    tq = 512
    kchunk = PAST_LEN
    heads = 2
    tiles_per_seq = DEC_SEQ // tq
    z_tile0 = N_P // tq
    gp_blk0 = ZB_GP_MLA * Z_BLK // (heads * LANE)
    return pl.pallas_call(
        functools.partial(_attn_sample_kernel, kchunk=kchunk, heads=heads),
        grid=(DEC_BATCH, MLA_HEADS // heads, tiles_per_seq),
        in_specs=[pl.BlockSpec((tq, heads * HEAD_SLOT), lambda b, hb, t: (b * tiles_per_seq + t, hb)),
                  pl.BlockSpec((DEC_SEQ, heads * HEAD_SLOT), lambda b, hb, t: (b, hb)),
                  pl.BlockSpec((DEC_SEQ, heads * LANE), lambda b, hb, t: (b, hb)),
                  pl.BlockSpec((1, 1, PAST_LEN, heads * HEAD_SLOT), lambda b, hb, t: (l, b, 0, hb)),
                  pl.BlockSpec((1, 1, PAST_LEN, heads * LANE), lambda b, hb, t: (l, b, 0, hb)),
                  pl.BlockSpec((tq, heads * LANE),
                               lambda b, hb, t: (z_tile0 + b * tiles_per_seq + t, gp_blk0 + hb))],
        out_specs=pl.BlockSpec((tq, heads * LANE), lambda b, hb, t: (b * tiles_per_seq + t, hb)),
        out_shape=jax.ShapeDtypeStruct((N_S, MLA_W), BF16),
        compiler_params=_params(3),
        name="attn_sample",
    )(q, k, v, kc, vc, z)


def _sgu_kernel(u_ref, v_ref, gp_ref, g_ref, w_ref, b_ref, o_ref, *, chunks):
    for c in range(chunks):
        rows = slice(c * SGU_CHUNK, (c + 1) * SGU_CHUNK)
        vn = _rms(v_ref[rows, :], g_ref[...]).astype(BF16)
        for g in range(SGU_GROUPS):
            cols = slice(g * LANE, (g + 1) * LANE)
            s = jnp.dot(w_ref[0, g], vn[:, cols], preferred_element_type=F32) + b_ref[0, :, cols]
            o_ref[rows, cols] = (gp_ref[rows, cols] * (u_ref[rows, cols] * s)).astype(BF16)


def _sgu(z, g_sgu_l, w_sgu, b_sgu_exp, l):
    tm = 512
    return pl.pallas_call(
        functools.partial(_sgu_kernel, chunks=tm // SGU_CHUNK),
        grid=(N_TOK // tm,),
        in_specs=[pl.BlockSpec((tm, Z_BLK), lambda i: (i, ZB_U)),
                  pl.BlockSpec((tm, Z_BLK), lambda i: (i, ZB_VS)),
                  pl.BlockSpec((tm, Z_BLK), lambda i: (i, ZB_GP_SGU)),
                  pl.BlockSpec((1, SGU_W), lambda i: (0, 0)),
                  pl.BlockSpec((1, SGU_GROUPS, SGU_CHUNK, SGU_CHUNK), lambda i: (l, 0, 0, 0)),
                  pl.BlockSpec((1, SGU_CHUNK, SGU_W), lambda i: (l, 0, 0))],
        out_specs=pl.BlockSpec((tm, SGU_W), lambda i: (i, 0)),
        out_shape=jax.ShapeDtypeStruct((N_TOK, SGU_W), BF16),
        compiler_params=_params(1),
        name="sgu",
    )(z, z, z, g_sgu_l, w_sgu, b_sgu_exp)


def _log_sigmoid(x):
    return -(jnp.maximum(-x, 0.0) + jnp.log1p(jnp.exp(-jnp.abs(x))))


def _ret_kernel(q_ref, k_ref, v_ref, gp_ref, dec_ref, g_ref, *rest, n_chunks, heads, has_state, emit_state):
    rest = list(rest)
    r0_ref = rest.pop(0) if has_state else None
    rout_ref = rest.pop() if emit_state else None
    o_ref = rest.pop()
    C = RET_CHUNK
    shape = (C, C)
    ii = lax.broadcasted_iota(jnp.int32, shape, 0).astype(F32)
    jj = lax.broadcasted_iota(jnp.int32, shape, 1).astype(F32)
    lower = ii >= jj
    k_scale = RET_DK ** -0.5

    for hd in range(heads):
        cols = slice(hd * LANE, (hd + 1) * LANE)
        lg_f = jnp.tile(_log_sigmoid(dec_ref[0, 0, hd]), (C // 8, 1))
        lg_b = jnp.tile(_log_sigmoid(dec_ref[0, 1, hd]), (C // 8, 1))
        dec = jnp.where(lower, jnp.exp(jnp.where(lower, ii - jj, 0.0) * lg_f),
                        jnp.exp(jnp.where(lower, 0.0, jj - ii) * lg_b))
        xi_f = jnp.exp((ii + 1.0) * lg_f)
        xi_b = jnp.exp(((C - 1.0 - ii) + 1.0) * lg_b)
        zeta_f = jnp.exp((C - 1.0 - jj) * lg_f)
        zeta_b = jnp.exp((C - 1.0 - (C - 1.0 - jj)) * lg_b)
        gc_f = jnp.exp(C * lg_f)
        gc_b = jnp.exp(C * lg_b)

        def rows_of(c):
            return slice(c * C, (c + 1) * C)

        kvs = []
        for c in range(n_chunks):
            k_t = jnp.transpose(k_ref[rows_of(c), cols] * k_scale)
            kz = jnp.concatenate([k_t * zeta_f, k_t * zeta_b], axis=0).astype(BF16)
            kvs.append(jnp.dot(kz, v_ref[rows_of(c), cols].astype(BF16), preferred_element_type=F32))

        rfs, rbs = [None] * n_chunks, [None] * n_chunks
        r_f = r0_ref[0, 0, 0, hd] if has_state else jnp.zeros(shape, F32)
        for c in range(n_chunks):
            rfs[c] = r_f.astype(BF16)
            r_f = gc_f * r_f + kvs[c][:C, :]
        r_b = r0_ref[0, 0, 1, hd] if has_state else jnp.zeros(shape, F32)
        for c in reversed(range(n_chunks)):
            rbs[c] = r_b.astype(BF16)
            r_b = gc_b * r_b + kvs[c][C:, :]
        if emit_state:
            for s in range(rout_ref.shape[1]):
                rout_ref[0, s, 0, hd] = r_f if s == 0 else jnp.zeros(shape, F32)
                rout_ref[0, s, 1, hd] = r_b if s == 0 else jnp.zeros(shape, F32)

        for c in range(n_chunks):
            q = q_ref[rows_of(c), cols].astype(BF16)
            k = (k_ref[rows_of(c), cols] * k_scale).astype(BF16)
            att = lax.dot_general(q, k, _NT, preferred_element_type=F32) * dec
            o = jnp.dot(att.astype(BF16), v_ref[rows_of(c), cols].astype(BF16), preferred_element_type=F32)
            use_f = has_state or c > 0
            use_b = has_state or c < n_chunks - 1
            if use_f and use_b:
                inter = jnp.dot(q, jnp.concatenate([rfs[c], rbs[c]], axis=1), preferred_element_type=F32)
                o = o + inter[:, :C] * xi_f + inter[:, C:] * xi_b
            elif use_f:
                o = o + jnp.dot(q, rfs[c], preferred_element_type=F32) * xi_f
            elif use_b:
                o = o + jnp.dot(q, rbs[c], preferred_element_type=F32) * xi_b
            mu = jnp.mean(o, axis=-1, keepdims=True)
            var = jnp.mean(jnp.square(o - mu), axis=-1, keepdims=True)
            y = (o - mu) * lax.rsqrt(var + EPS) * g_ref[:, cols]
            o_ref[rows_of(c), cols] = (gp_ref[rows_of(c), cols] * y).astype(BF16)


def _retention(z, decay_b, g_ret_l, state_ret, l, group, stack=None):
    seq = SEQ if group == 0 else DEC_SEQ
    nb = BATCH if group == 0 else DEC_BATCH
    heads = RET_HEADS if group == 0 else 2
    hblk = RET_HEADS // heads
    b0 = 0 if group == 0 else N_P // seq
    has_state = group == 1
    emit_state = group == 0
    n_chunks = seq // RET_CHUNK
    width = heads * LANE

    def zspec(zb):
        return pl.BlockSpec((seq, width), lambda b, hb: (b0 + b, zb * Z_BLK // width + hb))

    in_specs = [zspec(ZB_RQ), zspec(ZB_RK), zspec(ZB_RV), zspec(ZB_GP_RET),
                pl.BlockSpec((1, 2, heads, 8, LANE), lambda b, hb: (l, 0, hb, 0, 0)),
                pl.BlockSpec((1, width), lambda b, hb: (0, hb))]
    args = [z, z, z, z, decay_b, g_ret_l]
    state_block = (1, 1, 2, heads, RET_DK, RET_DV)
    if has_state:
        in_specs.append(pl.BlockSpec(state_block, lambda b, hb: (b, l, 0, hb, 0, 0)))
        args.append(state_ret)
    out_specs = [pl.BlockSpec((seq, width), lambda b, hb: (b, hb))]
    out_shape = [jax.ShapeDtypeStruct((nb * seq, RET_W), BF16)]
    aliases = {}
    if emit_state:
        assert (l == 0) == (stack is None)
        slices = DEPTH if stack is None else 1
        out_specs.append(pl.BlockSpec((1, slices) + state_block[2:], lambda b, hb: (b, l, 0, hb, 0, 0)))
        out_shape.append(jax.ShapeDtypeStruct((nb, DEPTH, 2, RET_HEADS, RET_DK, RET_DV), F32))
        if stack is not None:
            aliases = {len(args): 1}
            in_specs.append(pl.BlockSpec(memory_space=pl.ANY))
            args.append(stack)
    return pl.pallas_call(
        functools.partial(_ret_kernel, n_chunks=n_chunks, heads=heads, has_state=has_state,
                          emit_state=emit_state),
        grid=(nb, hblk),
        in_specs=in_specs,
        out_specs=out_specs,
        out_shape=out_shape,
        input_output_aliases=aliases,
        compiler_params=_params(2),
        name="ret_sample" if has_state else "ret_prompt",
    )(*args)


def _tail_kernel(ap_m, as_m, a_sgu, ap_r, as_r, m0, m1, m2, m3, m4, m5, wm_ref, ws_ref, wr_ref, wo_ref,
                 gpost_ref, mod_ref, *rest, tiles_p, first, last):
    i = pl.program_id(0)
    is_p = i < tiles_p
    rest = list(rest)
    if first:
        x = jnp.where(is_p, rest.pop(0)[...], rest.pop(0)[...])
    else:
        x = rest.pop(0)[...]
    a_mla = jnp.where(is_p, ap_m[...], as_m[...])
    a_ret = jnp.where(is_p, ap_r[...], as_r[...])
    y_mla = jnp.dot(a_mla, wm_ref[0], preferred_element_type=F32)
    y = jnp.concatenate([m0[...], m1[...]], axis=1) * y_mla
    y_sgu = jnp.dot(a_sgu[...], ws_ref[0], preferred_element_type=F32)
    y = y + jnp.concatenate([m2[...], m3[...]], axis=1) * y_sgu
    y_ret = jnp.dot(a_ret, wr_ref[0], preferred_element_type=F32)
    y = y + jnp.concatenate([m4[...], m5[...]], axis=1) * y_ret
    out = jnp.dot(y.astype(BF16), wo_ref[0], preferred_element_type=F32)
    x_new = x + mod_ref[0, 2:3, :] * _rms(out, gpost_ref[...])
    if last:
        yp_ref, ys_ref = rest

        @pl.when(is_p)
        def _():
            yp_ref[...] = x_new

        @pl.when(jnp.logical_not(is_p))
        def _():
            ys_ref[...] = x_new
    else:
        gpre_ref, modn_ref, x_out, h_out = rest
        x_out[...] = x_new
        h_out[...] = _prenorm_value(x_new, gpre_ref[...], modn_ref)


def _tail(a_mla_p, a_mla_s, a_sgu, a_ret_p, a_ret_s, z, w_br_mla, w_br_sgu, w_br_ret, w_out, xs, g_post_l,
          g_pre_next, mod3, l):
    tm = 256
    tiles_p = N_P // tm
    last = g_pre_next is None
    first = len(xs) == 2

    def p_spec():
        return pl.BlockSpec((tm, 1024), lambda i: (jnp.minimum(i, tiles_p - 1), 0))

    def s_spec():
        return pl.BlockSpec((tm, 1024), lambda i: (jnp.maximum(i - tiles_p, 0), 0))

    def w_spec(rows):
        return pl.BlockSpec((1, rows, D_MODEL), lambda i: (l, 0, 0), pipeline_mode=pl.Buffered(1))

    def row_spec():
        return pl.BlockSpec((tm, D_MODEL), lambda i: (i, 0))

    in_specs = [p_spec(), s_spec(), pl.BlockSpec((tm, 1024), lambda i: (i, 0)), p_spec(), s_spec()]
    in_specs += [pl.BlockSpec((tm, Z_BLK), lambda i, k=k: (i, ZB_MERGE + k)) for k in range(2 * N_BRANCH)]
    in_specs += [w_spec(1024), w_spec(1024), w_spec(1024), w_spec(D_MODEL),
                 pl.BlockSpec((1, D_MODEL), lambda i: (0, 0)), _mod_spec(l, tm)]
    args = [a_mla_p, a_mla_s, a_sgu, a_ret_p, a_ret_s] + [z] * (2 * N_BRANCH)
    args += [w_br_mla, w_br_sgu, w_br_ret, w_out, g_post_l, mod3]
    if first:
        in_specs += [pl.BlockSpec((tm, D_MODEL), lambda i: (jnp.minimum(i, tiles_p - 1), 0)),
                     pl.BlockSpec((tm, D_MODEL), lambda i: (jnp.maximum(i - tiles_p, 0), 0))]
    else:
        in_specs += [row_spec()]
    args += list(xs)
    if last:
        out_specs = [pl.BlockSpec((tm, D_MODEL), lambda i: (jnp.minimum(i, tiles_p - 1), 0)),
                     pl.BlockSpec((tm, D_MODEL), lambda i: (jnp.maximum(i - tiles_p, 0), 0))]
        out_shape = [jax.ShapeDtypeStruct((N_P, D_MODEL), F32), jax.ShapeDtypeStruct((N_S, D_MODEL), F32)]
    else:
        in_specs += [pl.BlockSpec((1, D_MODEL), lambda i: (0, 0)), _mod_spec(l + 1, tm)]
        args += [g_pre_next, mod3]
        out_specs = [row_spec(), row_spec()]
        out_shape = [jax.ShapeDtypeStruct((N_TOK, D_MODEL), F32), jax.ShapeDtypeStruct((N_TOK, D_MODEL), BF16)]
    return pl.pallas_call(
        functools.partial(_tail_kernel, tiles_p=tiles_p, first=first, last=last),
        grid=(N_TOK // tm,),
        in_specs=in_specs,
        out_specs=out_specs,
        out_shape=out_shape,
        compiler_params=_params(1),
        name="tail_last" if last else "tail",
    )(*args)


def _rope_tables():
    pos = jnp.arange(DEC_SEQ)
    row = (pos // GRID_W).astype(F32)
    col = (pos % GRID_W).astype(F32)
    n_freq = QK_ROPE // 4
    inv = ROPE_BASE ** (-jnp.arange(n_freq, dtype=F32) / n_freq)
    ang_r = row[:, None] * inv
    ang_c = col[:, None] * inv
    zeros = jnp.zeros((DEC_SEQ, n_freq), F32)
    pad = jnp.zeros((DEC_SEQ, LANE - QK_ROPE), F32)
    cr, sr, cc, sc = jnp.cos(ang_r), jnp.sin(ang_r), jnp.cos(ang_c), jnp.sin(ang_c)
    c = jnp.concatenate([cr, cr, cc, cc, pad], axis=1)
    s1 = jnp.concatenate([-sr, zeros, -sc, zeros, pad], axis=1)
    s2 = jnp.concatenate([zeros, sr, zeros, sc, pad], axis=1)
    return c, s1, s2


def kernel(x_prompt, x_sample, cache_ckv, cache_krope, state_ret, c, c_ctx, w_mod, b_mod, g_pre, g_post, w_in,
           g_q, g_kv, w_uq, w_ukv, g_sgu, w_sgu, b_sgu, ret_decay, g_ret, w_br_mla, w_br_sgu, w_br_ret, w_out):
    cond8 = jnp.concatenate([c_ctx[None, :], c, jnp.zeros((COND_ROWS - 1 - DEC_BATCH, D_MODEL), F32)], axis=0)
    mod3 = _modulation(cond8, w_mod, b_mod).reshape(DEPTH * COND_ROWS, 3, D_MODEL)
    w_in_t = jnp.swapaxes(w_in, 1, 2)
    w_kr = jnp.pad(w_in_t[:, KR_COL:KR_COL + QK_ROPE, :], ((0, 0), (0, LANE - QK_ROPE), (0, 0))).astype(BF16)
    w_uq_pad = jnp.pad(w_uq.reshape(DEPTH, Q_LORA, MLA_HEADS, QK_NOPE + QK_ROPE),
                       ((0, 0), (0, 0), (0, 0), (0, HEAD_SLOT - QK_NOPE - QK_ROPE))
                       ).reshape(DEPTH, Q_LORA, MLA_HEADS * HEAD_SLOT).astype(BF16)
    w_ukv_bf = w_ukv.astype(BF16)
    w_sgu_bf = w_sgu.astype(BF16)
    w_br = [w.astype(BF16) for w in (w_br_mla, w_br_sgu, w_br_ret)]
    w_out_bf = w_out.astype(BF16)
    b_sgu_exp = jnp.repeat(b_sgu, LANE, axis=2)
    decay_b = jnp.broadcast_to(ret_decay[:, :, :, None, None], (DEPTH, 2, RET_HEADS, 8, LANE))
    rope_tabs = _rope_tables()
    kc, vc = _ctxkv(cache_ckv, cache_krope, w_ukv_bf)

    xs = (x_prompt.reshape(N_P, D_MODEL), x_sample.reshape(N_S, D_MODEL))
    h = _prenorm(*xs, g_pre[0][None, :], mod3)
    cache_stacks = ret_stack = None
    for l in range(DEPTH):
        z = _inproj(h, w_in_t, l)
        q_p, k_p, v_p, *cache_stacks = _qkv(z, h, w_kr, w_uq_pad, w_ukv_bf, g_q[l][None, :], g_kv[l][None, :], None,
                                            l, 0, cache_stacks)
        a_mla_p = _attn_prompt(q_p, k_p, v_p, z)
        q_s, k_s, v_s = _qkv(z, h, w_kr, w_uq_pad, w_ukv_bf, g_q[l][None, :], g_kv[l][None, :], rope_tabs, l, 1)
        a_mla_s = _attn_sample(q_s, k_s, v_s, kc, vc, z, l)
        a_sgu = _sgu(z, g_sgu[l][None, :], w_sgu_bf, b_sgu_exp, l)
        a_ret_p, ret_stack = _retention(z, decay_b, g_ret[l][None, :], None, l, 0, ret_stack)
        (a_ret_s,) = _retention(z, decay_b, g_ret[l][None, :], state_ret, l, 1)
        g_pre_next = g_pre[l + 1][None, :] if l + 1 < DEPTH else None
        outs = _tail(a_mla_p, a_mla_s, a_sgu, a_ret_p, a_ret_s, z, *w_br, w_out_bf, xs, g_post[l][None, :],
                     g_pre_next, mod3, l)
        if l + 1 < DEPTH:
            x, h = outs
            xs = (x,)
        else:
            y_p, y_s = outs
    return (y_p.reshape(BATCH, SEQ, D_MODEL), y_s.reshape(DEC_BATCH, DEC_SEQ, D_MODEL), *cache_stacks, ret_stack)
```

```python
import functools

import jax
import jax.numpy as jnp
import numpy as np
from jax import lax
from jax.experimental import pallas as pl
from jax.experimental.pallas import tpu as pltpu

F32 = jnp.float32
BF16 = jnp.bfloat16

D_MODEL = 2048
BATCH = 16
SEQ = 256
DEPTH = 4
DEC_BATCH = 2
DEC_SEQ = 2048
PAST_LEN = 256
GRID_W = 64
EPS = 1e-6
ROPE_BASE = 10000.0

MLA_HEADS = 8
QK_NOPE = 128
QK_ROPE = 64
V_HEAD = 128
Q_LORA = 512
KV_LORA = 512
MLA_W = MLA_HEADS * V_HEAD
SGU_CHUNK = 128
SGU_GROUPS = 8
SGU_W = 1024
RET_HEADS = 8
RET_DK = 128
RET_DV = 128
RET_W = RET_HEADS * RET_DV
RET_CHUNK = 128
N_BRANCH = 3

LANE = 128
N_P = BATCH * SEQ
N_S = DEC_BATCH * DEC_SEQ
N_TOK = N_P + N_S
COND_ROWS = 8
HEAD_SLOT = 2 * LANE
KR_COL = Q_LORA + KV_LORA
Z_BLK = 1024
N_ZBLK = 15
ZB_GP_MLA, ZB_U, ZB_VS, ZB_GP_SGU, ZB_RQ, ZB_RK, ZB_RV, ZB_GP_RET, ZB_MERGE = 1, 2, 3, 4, 5, 6, 7, 8, 9
VMEM_LIMIT = 56 * 1024 * 1024
LOG2E = 1.4426950408889634

_NT = (((1,), (1,)), ((), ()))


def _params(n_axes, vmem=VMEM_LIMIT):
    return pltpu.CompilerParams(dimension_semantics=("arbitrary",) * n_axes, vmem_limit_bytes=vmem)


def _rms(x, g):
    return x * lax.rsqrt(jnp.mean(x * x, axis=-1, keepdims=True) + EPS) * g


def _silu(x):
    return x * jax.nn.sigmoid(x)


def _cond_row(tile, tm):
    return jnp.where(tile < N_P // tm, 0, 1 + (tile - N_P // tm) // (DEC_SEQ // tm))


def _mod_spec(l, tm):
    return pl.BlockSpec((1, 3, D_MODEL), lambda i: (l * COND_ROWS + _cond_row(i, tm), 0, 0))


def _prenorm_value(x, g, mod_ref):
    return (_rms(x, g) * (1.0 + mod_ref[0, 1:2, :]) + mod_ref[0, 0:1, :]).astype(BF16)


def _mod_kernel(cond_ref, w_ref, b_ref, o_ref):
    s = _silu(cond_ref[...]).astype(BF16)
    o_ref[0] = jnp.dot(s, w_ref[0].astype(BF16), preferred_element_type=F32) + b_ref[0]


def _modulation(cond8, w_mod, b_mod):
    tn = 1024
    return pl.pallas_call(
        _mod_kernel,
        grid=(DEPTH, 3 * D_MODEL // tn),
        in_specs=[pl.BlockSpec((COND_ROWS, D_MODEL), lambda l, n: (0, 0)),
                  pl.BlockSpec((1, D_MODEL, tn), lambda l, n: (l, 0, n)),
                  pl.BlockSpec((1, 1, tn), lambda l, n: (l, 0, n))],
        out_specs=pl.BlockSpec((1, COND_ROWS, tn), lambda l, n: (l, 0, n)),
        out_shape=jax.ShapeDtypeStruct((DEPTH, COND_ROWS, 3 * D_MODEL), F32),
        compiler_params=_params(2),
        name="modulation",
    )(cond8, w_mod, b_mod.reshape(DEPTH, 1, 3 * D_MODEL))


def _prenorm_kernel(xp_ref, xs_ref, g_ref, mod_ref, h_ref, *, tiles_p):
    x = jnp.where(pl.program_id(0) < tiles_p, xp_ref[...], xs_ref[...])
    h_ref[...] = _prenorm_value(x, g_ref[...], mod_ref)


def _prenorm(x_p, x_s, g_pre_l, mod3):
    tm = 512
    tiles_p = N_P // tm
    return pl.pallas_call(
        functools.partial(_prenorm_kernel, tiles_p=tiles_p),
        grid=(N_TOK // tm,),
        in_specs=[pl.BlockSpec((tm, D_MODEL), lambda i: (jnp.minimum(i, tiles_p - 1), 0)),
                  pl.BlockSpec((tm, D_MODEL), lambda i: (jnp.maximum(i - tiles_p, 0), 0)),
                  pl.BlockSpec((1, D_MODEL), lambda i: (0, 0)),
                  _mod_spec(0, tm)],
        out_specs=pl.BlockSpec((tm, D_MODEL), lambda i: (i, 0)),
        out_shape=jax.ShapeDtypeStruct((N_TOK, D_MODEL), BF16),
        compiler_params=_params(1),
        name="prenorm",
    )(x_p, x_s, g_pre_l, mod3)


def _inproj_kernel(h_ref, w_ref, o_ref, wbf_ref):
    j = pl.program_id(0)

    @pl.when(pl.program_id(1) == 0)
    def _():
        step = 128
        for r in range(0, Z_BLK, step):
            wbf_ref[r:r + step, :] = w_ref[0, r:r + step, :].astype(BF16)

    is_silu = jnp.logical_or(jnp.logical_or(j == ZB_GP_MLA, j == ZB_GP_SGU), j == ZB_GP_RET)
    acc = lax.dot_general(h_ref[...], wbf_ref[...], _NT, preferred_element_type=F32)
    sig = 0.5 * jnp.tanh(0.5 * acc) + 0.5
    o_ref[...] = jnp.where(is_silu, acc * sig, jnp.where(j >= ZB_MERGE, sig, acc))


def _inproj(h, w_in_t, l):
    tm = 1024
    return pl.pallas_call(
        _inproj_kernel,
        grid=(N_ZBLK, N_TOK // tm),
        in_specs=[pl.BlockSpec((tm, D_MODEL), lambda j, i: (i, 0)),
                  pl.BlockSpec((pl.Element(1), pl.Element(Z_BLK), pl.Element(D_MODEL)),
                               lambda j, i: (l, pl.multiple_of(Z_BLK * j + jnp.where(j > 0, QK_ROPE, 0), QK_ROPE),
                                             0))],
        out_specs=pl.BlockSpec((tm, Z_BLK), lambda j, i: (i, j)),
        out_shape=jax.ShapeDtypeStruct((N_TOK, N_ZBLK * Z_BLK), F32),
        scratch_shapes=[pltpu.VMEM((Z_BLK, D_MODEL), BF16)],
        compiler_params=_params(2),
        name="inproj",
    )(h, w_in_t)


def _rope(x, c, s1, s2):
    return x * c + pltpu.roll(x, LANE - QK_ROPE // 4, 1) * s1 + pltpu.roll(x, QK_ROPE // 4, 1) * s2


def _qkv_kernel(z_ref, h_ref, wkr_ref, wuq_ref, wukv_ref, gq_ref, gkv_ref, *rest, rope):
    if rope:
        c_ref, s1_ref, s2_ref, q_ref, k_ref, v_ref = rest
    else:
        q_ref, k_ref, v_ref, ckvn_ref, kr_ref = rest[-5:]
    cq = z_ref[:, :Q_LORA]
    ckv = z_ref[:, Q_LORA:]
    q = jnp.dot(_rms(cq, gq_ref[...]).astype(BF16), wuq_ref[0], preferred_element_type=F32)
    ckvn = _rms(ckv, gkv_ref[...])
    kv = jnp.dot(ckvn.astype(BF16), wukv_ref[0], preferred_element_type=F32)
    kr = lax.dot_general(h_ref[...], wkr_ref[0], _NT, preferred_element_type=F32)
    if rope:
        c, s1, s2 = c_ref[...], s1_ref[...], s2_ref[...]
        kr = _rope(kr, c, s1, s2)
    else:
        for b in range(ckvn_ref.shape[0]):
            rows = slice(b * SEQ, (b + 1) * SEQ)
            for s in range(ckvn_ref.shape[1]):
                ckvn_ref[b, s] = ckvn[rows, :] if s == 0 else jnp.zeros((SEQ, KV_LORA), F32)
                kr_ref[b, s] = kr[rows, :QK_ROPE] if s == 0 else jnp.zeros((SEQ, QK_ROPE), F32)
    kr_bf = kr.astype(BF16)
    for hd in range(MLA_HEADS):
        lo = hd * HEAD_SLOT
        q_ref[:, lo:lo + LANE] = q[:, lo:lo + LANE].astype(BF16)
        qr = q[:, lo + LANE:lo + HEAD_SLOT]
        if rope:
            qr = _rope(qr, c, s1, s2)
        q_ref[:, lo + LANE:lo + HEAD_SLOT] = qr.astype(BF16)
        k_ref[:, lo:lo + LANE] = kv[:, lo:lo + LANE].astype(BF16)
        k_ref[:, lo + LANE:lo + HEAD_SLOT] = kr_bf
        v_ref[:, hd * LANE:(hd + 1) * LANE] = kv[:, lo + LANE:lo + HEAD_SLOT].astype(BF16)


def _qkv(z, h, w_kr, w_uq_pad, w_ukv, g_q_l, g_kv_l, rope_tabs, l, group, stacks=None):
    tm = 512
    aliases = {}
    n_tok = N_P if group == 0 else N_S
    t0 = 0 if group == 0 else N_P // tm
    rope = group == 1
    in_specs = [pl.BlockSpec((tm, Z_BLK), lambda i: (i + t0, 0)),
                pl.BlockSpec((tm, D_MODEL), lambda i: (i + t0, 0)),
                pl.BlockSpec((1, LANE, D_MODEL), lambda i: (l, 0, 0)),
                pl.BlockSpec((1, Q_LORA, MLA_HEADS * HEAD_SLOT), lambda i: (l, 0, 0)),
                pl.BlockSpec((1, KV_LORA, MLA_HEADS * HEAD_SLOT), lambda i: (l, 0, 0)),
                pl.BlockSpec((1, Q_LORA), lambda i: (0, 0)),
                pl.BlockSpec((1, KV_LORA), lambda i: (0, 0))]
    args = [z, h, w_kr, w_uq_pad, w_ukv, g_q_l, g_kv_l]
    out_specs = [pl.BlockSpec((tm, MLA_HEADS * HEAD_SLOT), lambda i: (i, 0)),
                 pl.BlockSpec((tm, MLA_HEADS * HEAD_SLOT), lambda i: (i, 0)),
                 pl.BlockSpec((tm, MLA_W), lambda i: (i, 0))]
    out_shape = [jax.ShapeDtypeStruct((n_tok, MLA_HEADS * HEAD_SLOT), BF16),
                 jax.ShapeDtypeStruct((n_tok, MLA_HEADS * HEAD_SLOT), BF16),
                 jax.ShapeDtypeStruct((n_tok, MLA_W), BF16)]
    if rope:
        tiles_per_seq = DEC_SEQ // tm
        in_specs += [pl.BlockSpec((tm, LANE), lambda i: (i % tiles_per_seq, 0))] * 3
        args += list(rope_tabs)
    else:
        assert (l == 0) == (stacks is None)
        slices = DEPTH if stacks is None else 1
        out_specs += [pl.BlockSpec((tm // SEQ, slices, SEQ, KV_LORA), lambda i: (i, l, 0, 0)),
                      pl.BlockSpec((tm // SEQ, slices, SEQ, QK_ROPE), lambda i: (i, l, 0, 0))]
        out_shape += [jax.ShapeDtypeStruct((BATCH, DEPTH, SEQ, KV_LORA), F32),
                      jax.ShapeDtypeStruct((BATCH, DEPTH, SEQ, QK_ROPE), F32)]
        if stacks is not None:
            aliases = {len(args): 3, len(args) + 1: 4}
            in_specs += [pl.BlockSpec(memory_space=pl.ANY)] * 2
            args += list(stacks)
    return pl.pallas_call(
        functools.partial(_qkv_kernel, rope=rope),
        grid=(n_tok // tm,),
        in_specs=in_specs,
        out_specs=out_specs,
        out_shape=out_shape,
        input_output_aliases=aliases,
        compiler_params=_params(1),
        name="qkv_sample" if rope else "qkv_prompt",
    )(*args)


def _ctxkv_kernel(ckv_ref, kr_ref, w_ref, k_ref, v_ref):
    kv = jnp.dot(ckv_ref[0, 0].astype(BF16), w_ref[0], preferred_element_type=F32)
    kr = jnp.concatenate([kr_ref[0, 0], jnp.zeros((PAST_LEN, LANE - QK_ROPE), F32)], axis=1).astype(BF16)
    for hd in range(MLA_HEADS):
        lo = hd * HEAD_SLOT
        k_ref[0, 0, :, lo:lo + LANE] = kv[:, lo:lo + LANE].astype(BF16)
        k_ref[0, 0, :, lo + LANE:lo + HEAD_SLOT] = kr
        v_ref[0, 0, :, hd * LANE:(hd + 1) * LANE] = kv[:, lo + LANE:lo + HEAD_SLOT].astype(BF16)


def _ctxkv(cache_ckv, cache_krope, w_ukv):
    return pl.pallas_call(
        _ctxkv_kernel,
        grid=(DEPTH, DEC_BATCH),
        in_specs=[pl.BlockSpec((1, 1, PAST_LEN, KV_LORA), lambda l, b: (b, l, 0, 0)),
                  pl.BlockSpec((1, 1, PAST_LEN, QK_ROPE), lambda l, b: (b, l, 0, 0)),
                  pl.BlockSpec((1, KV_LORA, MLA_HEADS * HEAD_SLOT), lambda l, b: (l, 0, 0))],
        out_specs=[pl.BlockSpec((1, 1, PAST_LEN, MLA_HEADS * HEAD_SLOT), lambda l, b: (l, b, 0, 0)),
                   pl.BlockSpec((1, 1, PAST_LEN, MLA_W), lambda l, b: (l, b, 0, 0))],
        out_shape=[jax.ShapeDtypeStruct((DEPTH, DEC_BATCH, PAST_LEN, MLA_HEADS * HEAD_SLOT), BF16),
                   jax.ShapeDtypeStruct((DEPTH, DEC_BATCH, PAST_LEN, MLA_W), BF16)],
        compiler_params=_params(2),
        name="ctxkv",
    )(cache_ckv, cache_krope, w_ukv)


_EXP2_SCALE = (QK_NOPE + QK_ROPE) ** -0.5 * LOG2E


def _attn_prompt_kernel(q_ref, k_ref, v_ref, gp_ref, o_ref):
    for b in range(q_ref.shape[0] // SEQ):
        rows = slice(b * SEQ, (b + 1) * SEQ)
        for hd in range(MLA_HEADS):
            ks = slice(hd * HEAD_SLOT, (hd + 1) * HEAD_SLOT)
            vs = slice(hd * LANE, (hd + 1) * LANE)
            s = lax.dot_general(q_ref[rows, ks], k_ref[rows, ks], _NT, preferred_element_type=F32)
            e = jnp.exp2((s - jnp.max(s, axis=-1, keepdims=True)) * _EXP2_SCALE)
            p = (e * (1.0 / jnp.sum(e, axis=-1, keepdims=True))).astype(BF16)
            o = jnp.dot(p, v_ref[rows, vs], preferred_element_type=F32)
            o_ref[rows, vs] = (gp_ref[rows, vs] * o).astype(BF16)


def _attn_prompt(q, k, v, z):
    tm = 2 * SEQ
    return pl.pallas_call(
        _attn_prompt_kernel,
        grid=(N_P // tm,),
        in_specs=[pl.BlockSpec((tm, MLA_HEADS * HEAD_SLOT), lambda i: (i, 0)),
                  pl.BlockSpec((tm, MLA_HEADS * HEAD_SLOT), lambda i: (i, 0)),
                  pl.BlockSpec((tm, MLA_W), lambda i: (i, 0)),
                  pl.BlockSpec((tm, Z_BLK), lambda i: (i, ZB_GP_MLA))],
        out_specs=pl.BlockSpec((tm, MLA_W), lambda i: (i, 0)),
        out_shape=jax.ShapeDtypeStruct((N_P, MLA_W), BF16),
        compiler_params=_params(1),
        name="attn_prompt",
    )(q, k, v, z)


def _attn_sample_kernel(q_ref, k_ref, v_ref, kc_ref, vc_ref, gp_ref, o_ref, *, kchunk, heads):
    def lane_fold(x):
        out = x[:, :LANE]
        for c0 in range(LANE, kchunk, LANE):
            out = out + x[:, c0:c0 + LANE]
        return out

    for hd in range(heads):
        ks = slice(hd * HEAD_SLOT, (hd + 1) * HEAD_SLOT)
        vs = slice(hd * LANE, (hd + 1) * LANE)
        q = q_ref[:, ks]
        chunks = [(kc_ref[0, 0, :, ks], vc_ref[0, 0, :, vs])]
        chunks += [(k_ref[r:r + kchunk, ks], v_ref[r:r + kchunk, vs]) for r in range(0, DEC_SEQ, kchunk)]
        m = l = acc = None
        for k, v in chunks:
            s = lax.dot_general(q, k, _NT, preferred_element_type=F32)
            m_chunk = jnp.max(s, axis=-1, keepdims=True)
            if m is None:
                m = m_chunk
                e = jnp.exp2((s - m) * _EXP2_SCALE)
                l = lane_fold(e)
                acc = jnp.dot(e.astype(BF16), v, preferred_element_type=F32)
            else:
                m_new = jnp.maximum(m, m_chunk)
                alpha = jnp.exp2((m - m_new) * _EXP2_SCALE)
                e = jnp.exp2((s - m_new) * _EXP2_SCALE)
                l = alpha * l + lane_fold(e)
                acc = alpha * acc + jnp.dot(e.astype(BF16), v, preferred_element_type=F32)
                m = m_new
        inv = 1.0 / jnp.sum(l, axis=-1, keepdims=True)
        o_ref[:, vs] = (gp_ref[:, vs] * (acc * inv)).astype(BF16)


def _attn_sample(q, k, v, kc, vc, z, l):
    tq = 512
    kchunk = PAST_LEN
    heads = 2
    tiles_per_seq = DEC_SEQ // tq
    z_tile0 = N_P // tq
    gp_blk0 = ZB_GP_MLA * Z_BLK // (heads * LANE)
    return pl.pallas_call(
        functools.partial(_attn_sample_kernel, kchunk=kchunk, heads=heads),
        grid=(DEC_BATCH, MLA_HEADS // heads, tiles_per_seq),
        in_specs=[pl.BlockSpec((tq, heads * HEAD_SLOT), lambda b, hb, t: (b * tiles_per_seq + t, hb)),
                  pl.BlockSpec((DEC_SEQ, heads * HEAD_SLOT), lambda b, hb, t: (b, hb)),
                  pl.BlockSpec((DEC_SEQ, heads * LANE), lambda b, hb, t: (b, hb)),
                  pl.BlockSpec((1, 1, PAST_LEN, heads * HEAD_SLOT), lambda b, hb, t: (l, b, 0, hb)),
                  pl.BlockSpec((1, 1, PAST_LEN, heads * LANE), lambda b, hb, t: (l, b, 0, hb)),
                  pl.BlockSpec((tq, heads * LANE),
                               lambda b, hb, t: (z_tile0 + b * tiles_per_seq + t, gp_blk0 + hb))],
        out_specs=pl.BlockSpec((tq, heads * LANE), lambda b, hb, t: (b * tiles_per_seq + t, hb)),
        out_shape=jax.ShapeDtypeStruct((N_S, MLA_W), BF16),
        compiler_params=_params(3),
        name="attn_sample",
    )(q, k, v, kc, vc, z)


def _sgu_kernel(u_ref, v_ref, gp_ref, g_ref, w_ref, b_ref, o_ref, *, chunks):
    for c in range(chunks):
        rows = slice(c * SGU_CHUNK, (c + 1) * SGU_CHUNK)
        vn = _rms(v_ref[rows, :], g_ref[...]).astype(BF16)
        for g in range(SGU_GROUPS):
            cols = slice(g * LANE, (g + 1) * LANE)
            s = jnp.dot(w_ref[0, g], vn[:, cols], preferred_element_type=F32) + b_ref[0, :, cols]
            o_ref[rows, cols] = (gp_ref[rows, cols] * (u_ref[rows, cols] * s)).astype(BF16)


def _sgu(z, g_sgu_l, w_sgu, b_sgu_exp, l):
    tm = 1024
    return pl.pallas_call(
        functools.partial(_sgu_kernel, chunks=tm // SGU_CHUNK),
        grid=(N_TOK // tm,),
        in_specs=[pl.BlockSpec((tm, Z_BLK), lambda i: (i, ZB_U)),
                  pl.BlockSpec((tm, Z_BLK), lambda i: (i, ZB_VS)),
                  pl.BlockSpec((tm, Z_BLK), lambda i: (i, ZB_GP_SGU)),
                  pl.BlockSpec((1, SGU_W), lambda i: (0, 0)),
                  pl.BlockSpec((1, SGU_GROUPS, SGU_CHUNK, SGU_CHUNK), lambda i: (l, 0, 0, 0)),
                  pl.BlockSpec((1, SGU_CHUNK, SGU_W), lambda i: (l, 0, 0))],
        out_specs=pl.BlockSpec((tm, SGU_W), lambda i: (i, 0)),
        out_shape=jax.ShapeDtypeStruct((N_TOK, SGU_W), BF16),
        compiler_params=_params(1),
        name="sgu",
    )(z, z, z, g_sgu_l, w_sgu, b_sgu_exp)


def _log_sigmoid(x):
    return -(jnp.maximum(-x, 0.0) + jnp.log1p(jnp.exp(-jnp.abs(x))))


def _ret_kernel(q_ref, k_ref, v_ref, gp_ref, dec_ref, g_ref, *rest, n_chunks, heads, has_state, emit_state):
    rest = list(rest)
    r0_ref = rest.pop(0) if has_state else None
    rout_ref = rest.pop() if emit_state else None
    o_ref = rest.pop()
    C = RET_CHUNK
    shape = (C, C)
    ii = lax.broadcasted_iota(jnp.int32, shape, 0).astype(F32)
    jj = lax.broadcasted_iota(jnp.int32, shape, 1).astype(F32)
    lower = ii >= jj
    k_scale = RET_DK ** -0.5

    for hd in range(heads):
        cols = slice(hd * LANE, (hd + 1) * LANE)
        lg_f = jnp.tile(_log_sigmoid(dec_ref[0, 0, hd]), (C // 8, 1))
        lg_b = jnp.tile(_log_sigmoid(dec_ref[0, 1, hd]), (C // 8, 1))
        dec = jnp.where(lower, jnp.exp(jnp.where(lower, ii - jj, 0.0) * lg_f),
                        jnp.exp(jnp.where(lower, 0.0, jj - ii) * lg_b))
        xi_f = jnp.exp((ii + 1.0) * lg_f)
        xi_b = jnp.exp(((C - 1.0 - ii) + 1.0) * lg_b)
        zeta_f = jnp.exp((C - 1.0 - jj) * lg_f)
        zeta_b = jnp.exp((C - 1.0 - (C - 1.0 - jj)) * lg_b)
        gc_f = jnp.exp(C * lg_f)
        gc_b = jnp.exp(C * lg_b)

        def rows_of(c):
            return slice(c * C, (c + 1) * C)

        kvs = []
        for c in range(n_chunks):
            k_t = jnp.transpose(k_ref[rows_of(c), cols] * k_scale)
            kz = jnp.concatenate([k_t * zeta_f, k_t * zeta_b], axis=0).astype(BF16)
            kvs.append(jnp.dot(kz, v_ref[rows_of(c), cols].astype(BF16), preferred_element_type=F32))

        rfs, rbs = [None] * n_chunks, [None] * n_chunks
        r_f = r0_ref[0, 0, 0, hd] if has_state else jnp.zeros(shape, F32)
        for c in range(n_chunks):
            rfs[c] = r_f.astype(BF16)
            r_f = gc_f * r_f + kvs[c][:C, :]
        r_b = r0_ref[0, 0, 1, hd] if has_state else jnp.zeros(shape, F32)
        for c in reversed(range(n_chunks)):
            rbs[c] = r_b.astype(BF16)
            r_b = gc_b * r_b + kvs[c][C:, :]
        if emit_state:
            for s in range(rout_ref.shape[1]):
                rout_ref[0, s, 0, hd] = r_f if s == 0 else jnp.zeros(shape, F32)
                rout_ref[0, s, 1, hd] = r_b if s == 0 else jnp.zeros(shape, F32)

        for c in range(n_chunks):
            q = q_ref[rows_of(c), cols].astype(BF16)
            k = (k_ref[rows_of(c), cols] * k_scale).astype(BF16)
            att = lax.dot_general(q, k, _NT, preferred_element_type=F32) * dec
            o = jnp.dot(att.astype(BF16), v_ref[rows_of(c), cols].astype(BF16), preferred_element_type=F32)
            use_f = has_state or c > 0
            use_b = has_state or c < n_chunks - 1
            if use_f and use_b:
                inter = jnp.dot(q, jnp.concatenate([rfs[c], rbs[c]], axis=1), preferred_element_type=F32)
                o = o + inter[:, :C] * xi_f + inter[:, C:] * xi_b
            elif use_f:
                o = o + jnp.dot(q, rfs[c], preferred_element_type=F32) * xi_f
            elif use_b:
                o = o + jnp.dot(q, rbs[c], preferred_element_type=F32) * xi_b
            mu = jnp.mean(o, axis=-1, keepdims=True)
            var = jnp.mean(jnp.square(o - mu), axis=-1, keepdims=True)
            y = (o - mu) * lax.rsqrt(var + EPS) * g_ref[:, cols]
            o_ref[rows_of(c), cols] = (gp_ref[rows_of(c), cols] * y).astype(BF16)


def _retention(z, decay_b, g_ret_l, state_ret, l, group, stack=None):
    seq = SEQ if group == 0 else DEC_SEQ
    nb = BATCH if group == 0 else DEC_BATCH
    heads = RET_HEADS if group == 0 else 2
    hblk = RET_HEADS // heads
    b0 = 0 if group == 0 else N_P // seq
    has_state = group == 1
    emit_state = group == 0
    n_chunks = seq // RET_CHUNK
    width = heads * LANE

    def zspec(zb):
        return pl.BlockSpec((seq, width), lambda b, hb: (b0 + b, zb * Z_BLK // width + hb))

    in_specs = [zspec(ZB_RQ), zspec(ZB_RK), zspec(ZB_RV), zspec(ZB_GP_RET),
                pl.BlockSpec((1, 2, heads, 8, LANE), lambda b, hb: (l, 0, hb, 0, 0)),
                pl.BlockSpec((1, width), lambda b, hb: (0, hb))]
    args = [z, z, z, z, decay_b, g_ret_l]
    state_block = (1, 1, 2, heads, RET_DK, RET_DV)
    if has_state:
        in_specs.append(pl.BlockSpec(state_block, lambda b, hb: (b, l, 0, hb, 0, 0)))
        args.append(state_ret)
    out_specs = [pl.BlockSpec((seq, width), lambda b, hb: (b, hb))]
    out_shape = [jax.ShapeDtypeStruct((nb * seq, RET_W), BF16)]
    aliases = {}
    if emit_state:
        assert (l == 0) == (stack is None)
        slices = DEPTH if stack is None else 1
        out_specs.append(pl.BlockSpec((1, slices) + state_block[2:], lambda b, hb: (b, l, 0, hb, 0, 0)))
        out_shape.append(jax.ShapeDtypeStruct((nb, DEPTH, 2, RET_HEADS, RET_DK, RET_DV), F32))
        if stack is not None:
            aliases = {len(args): 1}
            in_specs.append(pl.BlockSpec(memory_space=pl.ANY))
            args.append(stack)
    return pl.pallas_call(
        functools.partial(_ret_kernel, n_chunks=n_chunks, heads=heads, has_state=has_state,
                          emit_state=emit_state),
        grid=(nb, hblk),
        in_specs=in_specs,
        out_specs=out_specs,
        out_shape=out_shape,
        input_output_aliases=aliases,
        compiler_params=_params(2),
        name="ret_sample" if has_state else "ret_prompt",
    )(*args)


def _tail_kernel(ap_m, as_m, a_sgu, ap_r, as_r, m_ref, wm_ref, ws_ref, wr_ref, wo_ref,
                 gpost_ref, mod_ref, *rest, tiles_p, first, last):
    i = pl.program_id(0)
    is_p = i < tiles_p
    rest = list(rest)
    if first:
        x = jnp.where(is_p, rest.pop(0)[...], rest.pop(0)[...])
    else:
        x = rest.pop(0)[...]
    a_mla = jnp.where(is_p, ap_m[...], as_m[...])
    a_ret = jnp.where(is_p, ap_r[...], as_r[...])
    y_mla = jnp.dot(a_mla, wm_ref[0], preferred_element_type=F32)
    y = m_ref[:, :D_MODEL] * y_mla
    y_sgu = jnp.dot(a_sgu[...], ws_ref[0], preferred_element_type=F32)
    y = y + m_ref[:, D_MODEL:2 * D_MODEL] * y_sgu
    y_ret = jnp.dot(a_ret, wr_ref[0], preferred_element_type=F32)
    y = y + m_ref[:, 2 * D_MODEL:] * y_ret
    out = jnp.dot(y.astype(BF16), wo_ref[0], preferred_element_type=F32)
    x_new = x + mod_ref[0, 2:3, :] * _rms(out, gpost_ref[...])
    if last:
        yp_ref, ys_ref = rest

        @pl.when(is_p)
        def _():
            yp_ref[...] = x_new

        @pl.when(jnp.logical_not(is_p))
        def _():
            ys_ref[...] = x_new
    else:
        gpre_ref, modn_ref, x_out, h_out = rest
        x_out[...] = x_new
        h_out[...] = _prenorm_value(x_new, gpre_ref[...], modn_ref)


def _tail(a_mla_p, a_mla_s, a_sgu, a_ret_p, a_ret_s, z, w_br_mla, w_br_sgu, w_br_ret, w_out, xs, g_post_l,
          g_pre_next, mod3, l):
    tm = 256
    tiles_p = N_P // tm
    last = g_pre_next is None
    first = len(xs) == 2

    def p_spec():
        return pl.BlockSpec((tm, 1024), lambda i: (jnp.minimum(i, tiles_p - 1), 0))

    def s_spec():
        return pl.BlockSpec((tm, 1024), lambda i: (jnp.maximum(i - tiles_p, 0), 0))

    def w_spec(rows):
        return pl.BlockSpec((1, rows, D_MODEL), lambda i: (l, 0, 0), pipeline_mode=pl.Buffered(1))

    def row_spec():
        return pl.BlockSpec((tm, D_MODEL), lambda i: (i, 0))

    in_specs = [p_spec(), s_spec(), pl.BlockSpec((tm, 1024), lambda i: (i, 0)), p_spec(), s_spec()]
    in_specs += [pl.BlockSpec((pl.Element(tm), pl.Element(N_BRANCH * D_MODEL)),
                              lambda i: (pl.multiple_of(i * tm, tm), ZB_MERGE * Z_BLK))]
    in_specs += [w_spec(1024), w_spec(1024), w_spec(1024), w_spec(D_MODEL),
                 pl.BlockSpec((1, D_MODEL), lambda i: (0, 0)), _mod_spec(l, tm)]
    args = [a_mla_p, a_mla_s, a_sgu, a_ret_p, a_ret_s, z]
    args += [w_br_mla, w_br_sgu, w_br_ret, w_out, g_post_l, mod3]
    if first:
        in_specs += [pl.BlockSpec((tm, D_MODEL), lambda i: (jnp.minimum(i, tiles_p - 1), 0)),
                     pl.BlockSpec((tm, D_MODEL), lambda i: (jnp.maximum(i - tiles_p, 0), 0))]
    else:
        in_specs += [row_spec()]
    args += list(xs)
    if last:
        out_specs = [pl.BlockSpec((tm, D_MODEL), lambda i: (jnp.minimum(i, tiles_p - 1), 0)),
                     pl.BlockSpec((tm, D_MODEL), lambda i: (jnp.maximum(i - tiles_p, 0), 0))]
        out_shape = [jax.ShapeDtypeStruct((N_P, D_MODEL), F32), jax.ShapeDtypeStruct((N_S, D_MODEL), F32)]
    else:
        in_specs += [pl.BlockSpec((1, D_MODEL), lambda i: (0, 0)), _mod_spec(l + 1, tm)]
        args += [g_pre_next, mod3]
        out_specs = [row_spec(), row_spec()]
        out_shape = [jax.ShapeDtypeStruct((N_TOK, D_MODEL), F32), jax.ShapeDtypeStruct((N_TOK, D_MODEL), BF16)]
    return pl.pallas_call(
        functools.partial(_tail_kernel, tiles_p=tiles_p, first=first, last=last),
        grid=(N_TOK // tm,),
        in_specs=in_specs,
        out_specs=out_specs,
        out_shape=out_shape,
        compiler_params=_params(1),
        name="tail_last" if last else "tail",
    )(*args)


def _rope_tables():
    f32 = np.float32
    pos = np.arange(DEC_SEQ)
    row = (pos // GRID_W).astype(f32)
    col = (pos % GRID_W).astype(f32)
    n_freq = QK_ROPE // 4
    inv = f32(ROPE_BASE) ** (-np.arange(n_freq, dtype=f32) / f32(n_freq))
    ang_r = row[:, None] * inv
    ang_c = col[:, None] * inv
    zeros = np.zeros((DEC_SEQ, n_freq), f32)
    pad = np.zeros((DEC_SEQ, LANE - QK_ROPE), f32)
    cr, sr, cc, sc = np.cos(ang_r), np.sin(ang_r), np.cos(ang_c), np.sin(ang_c)
    c = np.concatenate([cr, cr, cc, cc, pad], axis=1)
    s1 = np.concatenate([-sr, zeros, -sc, zeros, pad], axis=1)
    s2 = np.concatenate([zeros, sr, zeros, sc, pad], axis=1)
    return tuple(jnp.asarray(t.astype(f32)) for t in (c, s1, s2))


def kernel(x_prompt, x_sample, cache_ckv, cache_krope, state_ret, c, c_ctx, w_mod, b_mod, g_pre, g_post, w_in,
           g_q, g_kv, w_uq, w_ukv, g_sgu, w_sgu, b_sgu, ret_decay, g_ret, w_br_mla, w_br_sgu, w_br_ret, w_out):
    cond8 = jnp.concatenate([c_ctx[None, :], c, jnp.zeros((COND_ROWS - 1 - DEC_BATCH, D_MODEL), F32)], axis=0)
    mod3 = _modulation(cond8, w_mod, b_mod).reshape(DEPTH * COND_ROWS, 3, D_MODEL)
    w_in_t = jnp.swapaxes(w_in, 1, 2)
    w_kr = jnp.pad(w_in_t[:, KR_COL:KR_COL + QK_ROPE, :], ((0, 0), (0, LANE - QK_ROPE), (0, 0))).astype(BF16)
    w_uq_pad = jnp.pad(w_uq.reshape(DEPTH, Q_LORA, MLA_HEADS, QK_NOPE + QK_ROPE),
                       ((0, 0), (0, 0), (0, 0), (0, HEAD_SLOT - QK_NOPE - QK_ROPE))
                       ).reshape(DEPTH, Q_LORA, MLA_HEADS * HEAD_SLOT).astype(BF16)
    w_ukv_bf = w_ukv.astype(BF16)
    w_sgu_bf = w_sgu.astype(BF16)
    w_br = [w.astype(BF16) for w in (w_br_mla, w_br_sgu, w_br_ret)]
    w_out_bf = w_out.astype(BF16)
    b_sgu_exp = jnp.repeat(b_sgu, LANE, axis=2)
    decay_b = jnp.broadcast_to(ret_decay[:, :, :, None, None], (DEPTH, 2, RET_HEADS, 8, LANE))
    rope_tabs = _rope_tables()
    kc, vc = _ctxkv(cache_ckv, cache_krope, w_ukv_bf)

    xs = (x_prompt.reshape(N_P, D_MODEL), x_sample.reshape(N_S, D_MODEL))
    h = _prenorm(*xs, g_pre[0][None, :], mod3)
    cache_stacks = ret_stack = None
    for l in range(DEPTH):
        z = _inproj(h, w_in_t, l)
        q_p, k_p, v_p, *cache_stacks = _qkv(z, h, w_kr, w_uq_pad, w_ukv_bf, g_q[l][None, :], g_kv[l][None, :], None,
                                            l, 0, cache_stacks)
        a_mla_p = _attn_prompt(q_p, k_p, v_p, z)
        q_s, k_s, v_s = _qkv(z, h, w_kr, w_uq_pad, w_ukv_bf, g_q[l][None, :], g_kv[l][None, :], rope_tabs, l, 1)
        a_mla_s = _attn_sample(q_s, k_s, v_s, kc, vc, z, l)
        a_sgu = _sgu(z, g_sgu[l][None, :], w_sgu_bf, b_sgu_exp, l)
        a_ret_p, ret_stack = _retention(z, decay_b, g_ret[l][None, :], None, l, 0, ret_stack)
        (a_ret_s,) = _retention(z, decay_b, g_ret[l][None, :], state_ret, l, 1)
        g_pre_next = g_pre[l + 1][None, :] if l + 1 < DEPTH else None
        outs = _tail(a_mla_p, a_mla_s, a_sgu, a_ret_p, a_ret_s, z, *w_br, w_out_bf, xs, g_post[l][None, :],
                     g_pre_next, mod3, l)
        if l + 1 < DEPTH:
            x, h = outs
            xs = (x,)
        else:
            y_p, y_s = outs
    return (y_p.reshape(BATCH, SEQ, D_MODEL), y_s.reshape(DEC_BATCH, DEC_SEQ, D_MODEL), *cache_stacks, ret_stack)
```

```python
import functools

import jax
import jax.numpy as jnp
import numpy as np
from jax import lax
from jax.experimental import pallas as pl
from jax.experimental.pallas import tpu as pltpu

F32 = jnp.float32
BF16 = jnp.bfloat16

D_MODEL = 2048
BATCH = 16
SEQ = 256
DEPTH = 4
DEC_BATCH = 2
DEC_SEQ = 2048
PAST_LEN = 256
GRID_W = 64
EPS = 1e-6
ROPE_BASE = 10000.0

MLA_HEADS = 8
QK_NOPE = 128
QK_ROPE = 64
V_HEAD = 128
Q_LORA = 512
KV_LORA = 512
MLA_W = MLA_HEADS * V_HEAD
SGU_CHUNK = 128
SGU_GROUPS = 8
SGU_W = 1024
RET_HEADS = 8
RET_DK = 128
RET_DV = 128
RET_W = RET_HEADS * RET_DV
RET_CHUNK = 128
N_BRANCH = 3

LANE = 128
N_P = BATCH * SEQ
N_S = DEC_BATCH * DEC_SEQ
N_TOK = N_P + N_S
COND_ROWS = 8
HEAD_SLOT = 2 * LANE
KR_COL = Q_LORA + KV_LORA
Z_BLK = 1024
N_ZBLK = 15
ZB_GP_MLA, ZB_U, ZB_VS, ZB_GP_SGU, ZB_RQ, ZB_RK, ZB_RV, ZB_GP_RET, ZB_MERGE = 1, 2, 3, 4, 5, 6, 7, 8, 9
VMEM_LIMIT = 56 * 1024 * 1024
LOG2E = 1.4426950408889634

_NT = (((1,), (1,)), ((), ()))


def _params(n_axes, vmem=VMEM_LIMIT):
    return pltpu.CompilerParams(dimension_semantics=("arbitrary",) * n_axes, vmem_limit_bytes=vmem)


def _rms(x, g):
    return x * lax.rsqrt(jnp.mean(x * x, axis=-1, keepdims=True) + EPS) * g


def _silu(x):
    return x * jax.nn.sigmoid(x)


def _cond_row(tile, tm):
    return jnp.where(tile < N_P // tm, 0, 1 + (tile - N_P // tm) // (DEC_SEQ // tm))


def _mod_spec(l, tm):
    return pl.BlockSpec((1, 3, D_MODEL), lambda i: (l * COND_ROWS + _cond_row(i, tm), 0, 0))


def _prenorm_value(x, g, mod_ref):
    return (_rms(x, g) * (1.0 + mod_ref[0, 1:2, :]) + mod_ref[0, 0:1, :]).astype(BF16)


def _mod_kernel(cond_ref, w_ref, b_ref, o_ref):
    s = _silu(cond_ref[...]).astype(BF16)
    o_ref[0] = jnp.dot(s, w_ref[0].astype(BF16), preferred_element_type=F32) + b_ref[0]


def _modulation(cond8, w_mod, b_mod):
    tn = 1024
    return pl.pallas_call(
        _mod_kernel,
        grid=(DEPTH, 3 * D_MODEL // tn),
        in_specs=[pl.BlockSpec((COND_ROWS, D_MODEL), lambda l, n: (0, 0)),
                  pl.BlockSpec((1, D_MODEL, tn), lambda l, n: (l, 0, n)),
                  pl.BlockSpec((1, 1, tn), lambda l, n: (l, 0, n))],
        out_specs=pl.BlockSpec((1, COND_ROWS, tn), lambda l, n: (l, 0, n)),
        out_shape=jax.ShapeDtypeStruct((DEPTH, COND_ROWS, 3 * D_MODEL), F32),
        compiler_params=_params(2),
        name="modulation",
    )(cond8, w_mod, b_mod.reshape(DEPTH, 1, 3 * D_MODEL))


def _prenorm_kernel(xp_ref, xs_ref, g_ref, mod_ref, h_ref, *, tiles_p):
    x = jnp.where(pl.program_id(0) < tiles_p, xp_ref[...], xs_ref[...])
    h_ref[...] = _prenorm_value(x, g_ref[...], mod_ref)


def _prenorm(x_p, x_s, g_pre_l, mod3):
    tm = 512
    tiles_p = N_P // tm
    return pl.pallas_call(
        functools.partial(_prenorm_kernel, tiles_p=tiles_p),
        grid=(N_TOK // tm,),
        in_specs=[pl.BlockSpec((tm, D_MODEL), lambda i: (jnp.minimum(i, tiles_p - 1), 0)),
                  pl.BlockSpec((tm, D_MODEL), lambda i: (jnp.maximum(i - tiles_p, 0), 0)),
                  pl.BlockSpec((1, D_MODEL), lambda i: (0, 0)),
                  _mod_spec(0, tm)],
        out_specs=pl.BlockSpec((tm, D_MODEL), lambda i: (i, 0)),
        out_shape=jax.ShapeDtypeStruct((N_TOK, D_MODEL), BF16),
        compiler_params=_params(1),
        name="prenorm",
    )(x_p, x_s, g_pre_l, mod3)


def _inproj_kernel(h_ref, w_ref, o_ref, wbf_ref):
    j = pl.program_id(0)

    @pl.when(pl.program_id(1) == 0)
    def _():
        step = 128
        for r in range(0, Z_BLK, step):
            wbf_ref[r:r + step, :] = w_ref[0, r:r + step, :].astype(BF16)

    is_silu = jnp.logical_or(jnp.logical_or(j == ZB_GP_MLA, j == ZB_GP_SGU), j == ZB_GP_RET)
    acc = lax.dot_general(h_ref[...], wbf_ref[...], _NT, preferred_element_type=F32)
    sig = 0.5 * jnp.tanh(0.5 * acc) + 0.5
    o_ref[...] = jnp.where(is_silu, acc * sig, jnp.where(j >= ZB_MERGE, sig, acc))


def _inproj(h, w_in_t, l):
    tm = 1024
    return pl.pallas_call(
        _inproj_kernel,
        grid=(N_ZBLK, N_TOK // tm),
        in_specs=[pl.BlockSpec((tm, D_MODEL), lambda j, i: (i, 0)),
                  pl.BlockSpec((pl.Element(1), pl.Element(Z_BLK), pl.Element(D_MODEL)),
                               lambda j, i: (l, pl.multiple_of(Z_BLK * j + jnp.where(j > 0, QK_ROPE, 0), QK_ROPE),
                                             0))],
        out_specs=pl.BlockSpec((tm, Z_BLK), lambda j, i: (i, j)),
        out_shape=jax.ShapeDtypeStruct((N_TOK, N_ZBLK * Z_BLK), F32),
        scratch_shapes=[pltpu.VMEM((Z_BLK, D_MODEL), BF16)],
        compiler_params=_params(2),
        name="inproj",
    )(h, w_in_t)


def _rope(x, c, s1, s2):
    return x * c + pltpu.roll(x, LANE - QK_ROPE // 4, 1) * s1 + pltpu.roll(x, QK_ROPE // 4, 1) * s2


def _qkv_kernel(z_ref, h_ref, wkr_ref, wuq_ref, wukv_ref, gq_ref, gkv_ref, *rest, rope):
    if rope:
        c_ref, s1_ref, s2_ref, q_ref, k_ref, v_ref = rest
    else:
        q_ref, k_ref, v_ref, ckvn_ref, kr_ref = rest[-5:]
    cq = z_ref[:, :Q_LORA]
    ckv = z_ref[:, Q_LORA:]
    q = jnp.dot(_rms(cq, gq_ref[...]).astype(BF16), wuq_ref[0], preferred_element_type=F32)
    ckvn = _rms(ckv, gkv_ref[...])
    kv = jnp.dot(ckvn.astype(BF16), wukv_ref[0], preferred_element_type=F32)
    kr = lax.dot_general(h_ref[...], wkr_ref[0], _NT, preferred_element_type=F32)
    if rope:
        c, s1, s2 = c_ref[...], s1_ref[...], s2_ref[...]
        kr = _rope(kr, c, s1, s2)
    else:
        for b in range(ckvn_ref.shape[0]):
            rows = slice(b * SEQ, (b + 1) * SEQ)
            for s in range(ckvn_ref.shape[1]):
                ckvn_ref[b, s] = ckvn[rows, :] if s == 0 else jnp.zeros((SEQ, KV_LORA), F32)
                kr_ref[b, s] = kr[rows, :QK_ROPE] if s == 0 else jnp.zeros((SEQ, QK_ROPE), F32)
    kr_bf = kr.astype(BF16)
    for hd in range(MLA_HEADS):
        lo = hd * HEAD_SLOT
        q_ref[:, lo:lo + LANE] = q[:, lo:lo + LANE].astype(BF16)
        qr = q[:, lo + LANE:lo + HEAD_SLOT]
        if rope:
            qr = _rope(qr, c, s1, s2)
        q_ref[:, lo + LANE:lo + HEAD_SLOT] = qr.astype(BF16)
        k_ref[:, lo:lo + LANE] = kv[:, lo:lo + LANE].astype(BF16)
        k_ref[:, lo + LANE:lo + HEAD_SLOT] = kr_bf
        v_ref[:, hd * LANE:(hd + 1) * LANE] = kv[:, lo + LANE:lo + HEAD_SLOT].astype(BF16)


def _qkv(z, h, w_kr, w_uq_pad, w_ukv, g_q_l, g_kv_l, rope_tabs, l, group, stacks=None):
    tm = 512
    aliases = {}
    n_tok = N_P if group == 0 else N_S
    t0 = 0 if group == 0 else N_P // tm
    rope = group == 1
    in_specs = [pl.BlockSpec((tm, Z_BLK), lambda i: (i + t0, 0)),
                pl.BlockSpec((tm, D_MODEL), lambda i: (i + t0, 0)),
                pl.BlockSpec((1, LANE, D_MODEL), lambda i: (l, 0, 0)),
                pl.BlockSpec((1, Q_LORA, MLA_HEADS * HEAD_SLOT), lambda i: (l, 0, 0)),
                pl.BlockSpec((1, KV_LORA, MLA_HEADS * HEAD_SLOT), lambda i: (l, 0, 0)),
                pl.BlockSpec((1, Q_LORA), lambda i: (0, 0)),
                pl.BlockSpec((1, KV_LORA), lambda i: (0, 0))]
    args = [z, h, w_kr, w_uq_pad, w_ukv, g_q_l, g_kv_l]
    out_specs = [pl.BlockSpec((tm, MLA_HEADS * HEAD_SLOT), lambda i: (i, 0)),
                 pl.BlockSpec((tm, MLA_HEADS * HEAD_SLOT), lambda i: (i, 0)),
                 pl.BlockSpec((tm, MLA_W), lambda i: (i, 0))]
    out_shape = [jax.ShapeDtypeStruct((n_tok, MLA_HEADS * HEAD_SLOT), BF16),
                 jax.ShapeDtypeStruct((n_tok, MLA_HEADS * HEAD_SLOT), BF16),
                 jax.ShapeDtypeStruct((n_tok, MLA_W), BF16)]
    if rope:
        tiles_per_seq = DEC_SEQ // tm
        in_specs += [pl.BlockSpec((tm, LANE), lambda i: (i % tiles_per_seq, 0))] * 3
        args += list(rope_tabs)
    else:
        assert (l == 0) == (stacks is None)
        slices = DEPTH if stacks is None else 1
        out_specs += [pl.BlockSpec((tm // SEQ, slices, SEQ, KV_LORA), lambda i: (i, l, 0, 0)),
                      pl.BlockSpec((tm // SEQ, slices, SEQ, QK_ROPE), lambda i: (i, l, 0, 0))]
        out_shape += [jax.ShapeDtypeStruct((BATCH, DEPTH, SEQ, KV_LORA), F32),
                      jax.ShapeDtypeStruct((BATCH, DEPTH, SEQ, QK_ROPE), F32)]
        if stacks is not None:
            aliases = {len(args): 3, len(args) + 1: 4}
            in_specs += [pl.BlockSpec(memory_space=pl.ANY)] * 2
            args += list(stacks)
    return pl.pallas_call(
        functools.partial(_qkv_kernel, rope=rope),
        grid=(n_tok // tm,),
        in_specs=in_specs,
        out_specs=out_specs,
        out_shape=out_shape,
        input_output_aliases=aliases,
        compiler_params=_params(1),
        name="qkv_sample" if rope else "qkv_prompt",
    )(*args)


def _ctxkv_kernel(ckv_ref, kr_ref, w_ref, k_ref, v_ref):
    kv = jnp.dot(ckv_ref[0, 0].astype(BF16), w_ref[0], preferred_element_type=F32)
    kr = jnp.concatenate([kr_ref[0, 0], jnp.zeros((PAST_LEN, LANE - QK_ROPE), F32)], axis=1).astype(BF16)
    for hd in range(MLA_HEADS):
        lo = hd * HEAD_SLOT
        k_ref[0, 0, :, lo:lo + LANE] = kv[:, lo:lo + LANE].astype(BF16)
        k_ref[0, 0, :, lo + LANE:lo + HEAD_SLOT] = kr
        v_ref[0, 0, :, hd * LANE:(hd + 1) * LANE] = kv[:, lo + LANE:lo + HEAD_SLOT].astype(BF16)


def _ctxkv(cache_ckv, cache_krope, w_ukv):
    return pl.pallas_call(
        _ctxkv_kernel,
        grid=(DEPTH, DEC_BATCH),
        in_specs=[pl.BlockSpec((1, 1, PAST_LEN, KV_LORA), lambda l, b: (b, l, 0, 0)),
                  pl.BlockSpec((1, 1, PAST_LEN, QK_ROPE), lambda l, b: (b, l, 0, 0)),
                  pl.BlockSpec((1, KV_LORA, MLA_HEADS * HEAD_SLOT), lambda l, b: (l, 0, 0))],
        out_specs=[pl.BlockSpec((1, 1, PAST_LEN, MLA_HEADS * HEAD_SLOT), lambda l, b: (l, b, 0, 0)),
                   pl.BlockSpec((1, 1, PAST_LEN, MLA_W), lambda l, b: (l, b, 0, 0))],
        out_shape=[jax.ShapeDtypeStruct((DEPTH, DEC_BATCH, PAST_LEN, MLA_HEADS * HEAD_SLOT), BF16),
                   jax.ShapeDtypeStruct((DEPTH, DEC_BATCH, PAST_LEN, MLA_W), BF16)],
        compiler_params=_params(2),
        name="ctxkv",
    )(cache_ckv, cache_krope, w_ukv)


_EXP2_SCALE = (QK_NOPE + QK_ROPE) ** -0.5 * LOG2E


def _attn_prompt_kernel(q_ref, k_ref, v_ref, gp_ref, o_ref):
    for b in range(q_ref.shape[0] // SEQ):
        rows = slice(b * SEQ, (b + 1) * SEQ)
        for hd in range(MLA_HEADS):
            ks = slice(hd * HEAD_SLOT, (hd + 1) * HEAD_SLOT)
            vs = slice(hd * LANE, (hd + 1) * LANE)
            s = lax.dot_general(q_ref[rows, ks], k_ref[rows, ks], _NT, preferred_element_type=F32)
            e = jnp.exp2((s - jnp.max(s, axis=-1, keepdims=True)) * _EXP2_SCALE)
            p = (e * (1.0 / jnp.sum(e, axis=-1, keepdims=True))).astype(BF16)
            o = jnp.dot(p, v_ref[rows, vs], preferred_element_type=F32)
            o_ref[rows, vs] = (gp_ref[rows, vs] * o).astype(BF16)


def _attn_prompt(q, k, v, z):
    tm = 2 * SEQ
    return pl.pallas_call(
        _attn_prompt_kernel,
        grid=(N_P // tm,),
        in_specs=[pl.BlockSpec((tm, MLA_HEADS * HEAD_SLOT), lambda i: (i, 0)),
                  pl.BlockSpec((tm, MLA_HEADS * HEAD_SLOT), lambda i: (i, 0)),
                  pl.BlockSpec((tm, MLA_W), lambda i: (i, 0)),
                  pl.BlockSpec((tm, Z_BLK), lambda i: (i, ZB_GP_MLA))],
        out_specs=pl.BlockSpec((tm, MLA_W), lambda i: (i, 0)),
        out_shape=jax.ShapeDtypeStruct((N_P, MLA_W), BF16),
        compiler_params=_params(1),
        name="attn_prompt",
    )(q, k, v, z)


def _attn_sample_kernel(q_ref, k_ref, v_ref, kc_ref, vc_ref, gp_ref, o_ref, *, kchunk, heads):
    def lane_fold(x):
        out = x[:, :LANE]
        for c0 in range(LANE, kchunk, LANE):
            out = out + x[:, c0:c0 + LANE]
        return out

    for hd in range(heads):
        ks = slice(hd * HEAD_SLOT, (hd + 1) * HEAD_SLOT)
        vs = slice(hd * LANE, (hd + 1) * LANE)
        q = q_ref[:, ks]
        chunks = [(kc_ref[0, 0, :, ks], vc_ref[0, 0, :, vs])]
        chunks += [(k_ref[r:r + kchunk, ks], v_ref[r:r + kchunk, vs]) for r in range(0, DEC_SEQ, kchunk)]
        m = l = acc = None
        for k, v in chunks:
            s = lax.dot_general(q, k, _NT, preferred_element_type=F32)
            m_chunk = jnp.max(s, axis=-1, keepdims=True)
            if m is None:
                m = m_chunk
                e = jnp.exp2((s - m) * _EXP2_SCALE)
                l = lane_fold(e)
                acc = jnp.dot(e.astype(BF16), v, preferred_element_type=F32)
            else:
                m_new = jnp.maximum(m, m_chunk)
                alpha = jnp.exp2((m - m_new) * _EXP2_SCALE)
                e = jnp.exp2((s - m_new) * _EXP2_SCALE)
                l = alpha * l + lane_fold(e)
                acc = alpha * acc + jnp.dot(e.astype(BF16), v, preferred_element_type=F32)
                m = m_new
        inv = 1.0 / jnp.sum(l, axis=-1, keepdims=True)
        o_ref[:, vs] = (gp_ref[:, vs] * (acc * inv)).astype(BF16)


def _attn_sample(q, k, v, kc, vc, z, l):
    tq = 512
    kchunk = PAST_LEN
    heads = 2
    tiles_per_seq = DEC_SEQ // tq
    z_tile0 = N_P // tq
    gp_blk0 = ZB_GP_MLA * Z_BLK // (heads * LANE)
    return pl.pallas_call(
        functools.partial(_attn_sample_kernel, kchunk=kchunk, heads=heads),
        grid=(DEC_BATCH, MLA_HEADS // heads, tiles_per_seq),
        in_specs=[pl.BlockSpec((tq, heads * HEAD_SLOT), lambda b, hb, t: (b * tiles_per_seq + t, hb)),
                  pl.BlockSpec((DEC_SEQ, heads * HEAD_SLOT), lambda b, hb, t: (b, hb)),
                  pl.BlockSpec((DEC_SEQ, heads * LANE), lambda b, hb, t: (b, hb)),
                  pl.BlockSpec((1, 1, PAST_LEN, heads * HEAD_SLOT), lambda b, hb, t: (l, b, 0, hb)),
                  pl.BlockSpec((1, 1, PAST_LEN, heads * LANE), lambda b, hb, t: (l, b, 0, hb)),
                  pl.BlockSpec((tq, heads * LANE),
                               lambda b, hb, t: (z_tile0 + b * tiles_per_seq + t, gp_blk0 + hb))],
        out_specs=pl.BlockSpec((tq, heads * LANE), lambda b, hb, t: (b * tiles_per_seq + t, hb)),
        out_shape=jax.ShapeDtypeStruct((N_S, MLA_W), BF16),
        compiler_params=_params(3),
        name="attn_sample",
    )(q, k, v, kc, vc, z)


def _sgu_kernel(u_ref, v_ref, gp_ref, g_ref, w_ref, b_ref, o_ref, *, chunks):
    for c in range(chunks):
        rows = slice(c * SGU_CHUNK, (c + 1) * SGU_CHUNK)
        vn = _rms(v_ref[rows, :], g_ref[...]).astype(BF16)
        for g in range(SGU_GROUPS):
            cols = slice(g * LANE, (g + 1) * LANE)
            s = jnp.dot(w_ref[0, g], vn[:, cols], preferred_element_type=F32) + b_ref[0, :, cols]
            o_ref[rows, cols] = (gp_ref[rows, cols] * (u_ref[rows, cols] * s)).astype(BF16)


def _sgu(z, g_sgu_l, w_sgu, b_sgu_exp, l):
    tm = 1024
    return pl.pallas_call(
        functools.partial(_sgu_kernel, chunks=tm // SGU_CHUNK),
        grid=(N_TOK // tm,),
        in_specs=[pl.BlockSpec((tm, Z_BLK), lambda i: (i, ZB_U)),
                  pl.BlockSpec((tm, Z_BLK), lambda i: (i, ZB_VS)),
                  pl.BlockSpec((tm, Z_BLK), lambda i: (i, ZB_GP_SGU)),
                  pl.BlockSpec((1, SGU_W), lambda i: (0, 0)),
                  pl.BlockSpec((1, SGU_GROUPS, SGU_CHUNK, SGU_CHUNK), lambda i: (l, 0, 0, 0)),
                  pl.BlockSpec((1, SGU_CHUNK, SGU_W), lambda i: (l, 0, 0))],
        out_specs=pl.BlockSpec((tm, SGU_W), lambda i: (i, 0)),
        out_shape=jax.ShapeDtypeStruct((N_TOK, SGU_W), BF16),
        compiler_params=_params(1),
        name="sgu",
    )(z, z, z, g_sgu_l, w_sgu, b_sgu_exp)


def _log_sigmoid(x):
    return -(jnp.maximum(-x, 0.0) + jnp.log1p(jnp.exp(-jnp.abs(x))))


def _ret_kernel(q_ref, k_ref, v_ref, gp_ref, dec_ref, g_ref, *rest, block, n_chunks, heads, has_state,
                emit_state):
    rest = list(rest)
    r0_ref = rest.pop(0) if has_state else None
    rout_ref = rest.pop() if emit_state else None
    o_ref = rest.pop()
    C = block
    shape = (RET_DK, RET_DV)
    ii = lax.broadcasted_iota(jnp.int32, (C, C), 0).astype(F32)
    jj = lax.broadcasted_iota(jnp.int32, (C, C), 1).astype(F32)
    ri = lax.broadcasted_iota(jnp.int32, (C, RET_DV), 0).astype(F32)
    cj = lax.broadcasted_iota(jnp.int32, (RET_DK, C), 1).astype(F32)
    lower = ii >= jj
    k_scale = RET_DK ** -0.5

    for hd in range(heads):
        cols = slice(hd * LANE, (hd + 1) * LANE)
        lg_f8 = _log_sigmoid(dec_ref[0, 0, hd])
        lg_b8 = _log_sigmoid(dec_ref[0, 1, hd])

        def spread(x8, rows, width):
            return jnp.tile(x8, (rows // 8, width // LANE))

        dec = jnp.where(lower, jnp.exp(jnp.where(lower, ii - jj, 0.0) * spread(lg_f8, C, C)),
                        jnp.exp(jnp.where(lower, 0.0, jj - ii) * spread(lg_b8, C, C)))
        xi_f = jnp.exp((ri + 1.0) * spread(lg_f8, C, RET_DV))
        xi_b = jnp.exp(((C - 1.0 - ri) + 1.0) * spread(lg_b8, C, RET_DV))
        zeta_f = jnp.exp((C - 1.0 - cj) * spread(lg_f8, RET_DK, C))
        zeta_b = jnp.exp((C - 1.0 - (C - 1.0 - cj)) * spread(lg_b8, RET_DK, C))
        gc_f = jnp.exp(C * spread(lg_f8, RET_DK, RET_DV))
        gc_b = jnp.exp(C * spread(lg_b8, RET_DK, RET_DV))

        def rows_of(c):
            return slice(c * C, (c + 1) * C)

        kvs = []
        for c in range(n_chunks):
            k_t = jnp.transpose(k_ref[rows_of(c), cols] * k_scale)
            kz = jnp.concatenate([k_t * zeta_f, k_t * zeta_b], axis=0).astype(BF16)
            kvs.append(jnp.dot(kz, v_ref[rows_of(c), cols].astype(BF16), preferred_element_type=F32))

        rfs, rbs = [None] * n_chunks, [None] * n_chunks
        r_f = r0_ref[0, 0, 0, hd] if has_state else jnp.zeros(shape, F32)
        for c in range(n_chunks):
            rfs[c] = r_f.astype(BF16)
            r_f = gc_f * r_f + kvs[c][:RET_DK, :]
        r_b = r0_ref[0, 0, 1, hd] if has_state else jnp.zeros(shape, F32)
        for c in reversed(range(n_chunks)):
            rbs[c] = r_b.astype(BF16)
            r_b = gc_b * r_b + kvs[c][RET_DK:, :]
        if emit_state:
            for s in range(rout_ref.shape[1]):
                rout_ref[0, s, 0, hd] = r_f if s == 0 else jnp.zeros(shape, F32)
                rout_ref[0, s, 1, hd] = r_b if s == 0 else jnp.zeros(shape, F32)

        for c in range(n_chunks):
            q = q_ref[rows_of(c), cols].astype(BF16)
            k = (k_ref[rows_of(c), cols] * k_scale).astype(BF16)
            att = lax.dot_general(q, k, _NT, preferred_element_type=F32) * dec
            o = jnp.dot(att.astype(BF16), v_ref[rows_of(c), cols].astype(BF16), preferred_element_type=F32)
            use_f = has_state or c > 0
            use_b = has_state or c < n_chunks - 1
            if use_f and use_b:
                inter = jnp.dot(q, jnp.concatenate([rfs[c], rbs[c]], axis=1), preferred_element_type=F32)
                o = o + inter[:, :RET_DV] * xi_f + inter[:, RET_DV:] * xi_b
            elif use_f:
                o = o + jnp.dot(q, rfs[c], preferred_element_type=F32) * xi_f
            elif use_b:
                o = o + jnp.dot(q, rbs[c], preferred_element_type=F32) * xi_b
            mu = jnp.mean(o, axis=-1, keepdims=True)
            var = jnp.mean(jnp.square(o - mu), axis=-1, keepdims=True)
            y = (o - mu) * lax.rsqrt(var + EPS) * g_ref[:, cols]
            o_ref[rows_of(c), cols] = (gp_ref[rows_of(c), cols] * y).astype(BF16)


def _retention(z, decay_b, g_ret_l, state_ret, l, group, stack=None):
    seq = SEQ if group == 0 else DEC_SEQ
    nb = BATCH if group == 0 else DEC_BATCH
    heads = RET_HEADS if group == 0 else 2
    hblk = RET_HEADS // heads
    b0 = 0 if group == 0 else N_P // seq
    has_state = group == 1
    emit_state = group == 0
    block = 2 * RET_CHUNK
    n_chunks = seq // block
    width = heads * LANE

    def zspec(zb):
        return pl.BlockSpec((seq, width), lambda b, hb: (b0 + b, zb * Z_BLK // width + hb))

    in_specs = [zspec(ZB_RQ), zspec(ZB_RK), zspec(ZB_RV), zspec(ZB_GP_RET),
                pl.BlockSpec((1, 2, heads, 8, LANE), lambda b, hb: (l, 0, hb, 0, 0)),
                pl.BlockSpec((1, width), lambda b, hb: (0, hb))]
    args = [z, z, z, z, decay_b, g_ret_l]
    state_block = (1, 1, 2, heads, RET_DK, RET_DV)
    if has_state:
        in_specs.append(pl.BlockSpec(state_block, lambda b, hb: (b, l, 0, hb, 0, 0)))
        args.append(state_ret)
    out_specs = [pl.BlockSpec((seq, width), lambda b, hb: (b, hb))]
    out_shape = [jax.ShapeDtypeStruct((nb * seq, RET_W), BF16)]
    aliases = {}
    if emit_state:
        assert (l == 0) == (stack is None)
        slices = DEPTH if stack is None else 1
        out_specs.append(pl.BlockSpec((1, slices) + state_block[2:], lambda b, hb: (b, l, 0, hb, 0, 0)))
        out_shape.append(jax.ShapeDtypeStruct((nb, DEPTH, 2, RET_HEADS, RET_DK, RET_DV), F32))
        if stack is not None:
            aliases = {len(args): 1}
            in_specs.append(pl.BlockSpec(memory_space=pl.ANY))
            args.append(stack)
    return pl.pallas_call(
        functools.partial(_ret_kernel, block=block, n_chunks=n_chunks, heads=heads, has_state=has_state,
                          emit_state=emit_state),
        grid=(nb, hblk),
        in_specs=in_specs,
        out_specs=out_specs,
        out_shape=out_shape,
        input_output_aliases=aliases,
        compiler_params=_params(2),
        name="ret_sample" if has_state else "ret_prompt",
    )(*args)


def _tail_kernel(ap_m, as_m, a_sgu, ap_r, as_r, m_ref, wm_ref, ws_ref, wr_ref, wo_ref,
                 gpost_ref, mod_ref, *rest, tiles_p, first, last):
    i = pl.program_id(0)
    is_p = i < tiles_p
    rest = list(rest)
    if first:
        x = jnp.where(is_p, rest.pop(0)[...], rest.pop(0)[...])
    else:
        x = rest.pop(0)[...]
    a_mla = jnp.where(is_p, ap_m[...], as_m[...])
    a_ret = jnp.where(is_p, ap_r[...], as_r[...])
    y_mla = jnp.dot(a_mla, wm_ref[0], preferred_element_type=F32)
    y = m_ref[:, :D_MODEL] * y_mla
    y_sgu = jnp.dot(a_sgu[...], ws_ref[0], preferred_element_type=F32)
    y = y + m_ref[:, D_MODEL:2 * D_MODEL] * y_sgu
    y_ret = jnp.dot(a_ret, wr_ref[0], preferred_element_type=F32)
    y = y + m_ref[:, 2 * D_MODEL:] * y_ret
    out = jnp.dot(y.astype(BF16), wo_ref[0], preferred_element_type=F32)
    x_new = x + mod_ref[0, 2:3, :] * _rms(out, gpost_ref[...])
    if last:
        yp_ref, ys_ref = rest

        @pl.when(is_p)
        def _():
            yp_ref[...] = x_new

        @pl.when(jnp.logical_not(is_p))
        def _():
            ys_ref[...] = x_new
    else:
        gpre_ref, modn_ref, x_out, h_out = rest
        x_out[...] = x_new
        h_out[...] = _prenorm_value(x_new, gpre_ref[...], modn_ref)


def _tail(a_mla_p, a_mla_s, a_sgu, a_ret_p, a_ret_s, z, w_br_mla, w_br_sgu, w_br_ret, w_out, xs, g_post_l,
          g_pre_next, mod3, l):
    tm = 256
    tiles_p = N_P // tm
    last = g_pre_next is None
    first = len(xs) == 2

    def p_spec():
        return pl.BlockSpec((tm, 1024), lambda i: (jnp.minimum(i, tiles_p - 1), 0))

    def s_spec():
        return pl.BlockSpec((tm, 1024), lambda i: (jnp.maximum(i - tiles_p, 0), 0))

    def w_spec(rows):
        return pl.BlockSpec((1, rows, D_MODEL), lambda i: (l, 0, 0), pipeline_mode=pl.Buffered(1))

    def row_spec():
        return pl.BlockSpec((tm, D_MODEL), lambda i: (i, 0))

    in_specs = [p_spec(), s_spec(), pl.BlockSpec((tm, 1024), lambda i: (i, 0)), p_spec(), s_spec()]
    in_specs += [pl.BlockSpec((pl.Element(tm), pl.Element(N_BRANCH * D_MODEL)),
                              lambda i: (pl.multiple_of(i * tm, tm), ZB_MERGE * Z_BLK))]
    in_specs += [w_spec(1024), w_spec(1024), w_spec(1024), w_spec(D_MODEL),
                 pl.BlockSpec((1, D_MODEL), lambda i: (0, 0)), _mod_spec(l, tm)]
    args = [a_mla_p, a_mla_s, a_sgu, a_ret_p, a_ret_s, z]
    args += [w_br_mla, w_br_sgu, w_br_ret, w_out, g_post_l, mod3]
    if first:
        in_specs += [pl.BlockSpec((tm, D_MODEL), lambda i: (jnp.minimum(i, tiles_p - 1), 0)),
                     pl.BlockSpec((tm, D_MODEL), lambda i: (jnp.maximum(i - tiles_p, 0), 0))]
    else:
        in_specs += [row_spec()]
    args += list(xs)
    if last:
        out_specs = [pl.BlockSpec((tm, D_MODEL), lambda i: (jnp.minimum(i, tiles_p - 1), 0)),
                     pl.BlockSpec((tm, D_MODEL), lambda i: (jnp.maximum(i - tiles_p, 0), 0))]
        out_shape = [jax.ShapeDtypeStruct((N_P, D_MODEL), F32), jax.ShapeDtypeStruct((N_S, D_MODEL), F32)]
    else:
        in_specs += [pl.BlockSpec((1, D_MODEL), lambda i: (0, 0)), _mod_spec(l + 1, tm)]
        args += [g_pre_next, mod3]
        out_specs = [row_spec(), row_spec()]
        out_shape = [jax.ShapeDtypeStruct((N_TOK, D_MODEL), F32), jax.ShapeDtypeStruct((N_TOK, D_MODEL), BF16)]
    return pl.pallas_call(
        functools.partial(_tail_kernel, tiles_p=tiles_p, first=first, last=last),
        grid=(N_TOK // tm,),
        in_specs=in_specs,
        out_specs=out_specs,
        out_shape=out_shape,
        compiler_params=_params(1),
        name="tail_last" if last else "tail",
    )(*args)


def _rope_tables():
    f32 = np.float32
    pos = np.arange(DEC_SEQ)
    row = (pos // GRID_W).astype(f32)
    col = (pos % GRID_W).astype(f32)
    n_freq = QK_ROPE // 4
    inv = f32(ROPE_BASE) ** (-np.arange(n_freq, dtype=f32) / f32(n_freq))
    ang_r = row[:, None] * inv
    ang_c = col[:, None] * inv
    zeros = np.zeros((DEC_SEQ, n_freq), f32)
    pad = np.zeros((DEC_SEQ, LANE - QK_ROPE), f32)
    cr, sr, cc, sc = np.cos(ang_r), np.sin(ang_r), np.cos(ang_c), np.sin(ang_c)
    c = np.concatenate([cr, cr, cc, cc, pad], axis=1)
    s1 = np.concatenate([-sr, zeros, -sc, zeros, pad], axis=1)
    s2 = np.concatenate([zeros, sr, zeros, sc, pad], axis=1)
    return tuple(jnp.asarray(t.astype(f32)) for t in (c, s1, s2))


def kernel(x_prompt, x_sample, cache_ckv, cache_krope, state_ret, c, c_ctx, w_mod, b_mod, g_pre, g_post, w_in,
           g_q, g_kv, w_uq, w_ukv, g_sgu, w_sgu, b_sgu, ret_decay, g_ret, w_br_mla, w_br_sgu, w_br_ret, w_out):
    cond8 = jnp.concatenate([c_ctx[None, :], c, jnp.zeros((COND_ROWS - 1 - DEC_BATCH, D_MODEL), F32)], axis=0)
    mod3 = _modulation(cond8, w_mod, b_mod).reshape(DEPTH * COND_ROWS, 3, D_MODEL)
    w_in_t = jnp.swapaxes(w_in, 1, 2)
    w_kr = jnp.pad(w_in_t[:, KR_COL:KR_COL + QK_ROPE, :], ((0, 0), (0, LANE - QK_ROPE), (0, 0))).astype(BF16)
    w_uq_pad = jnp.pad(w_uq.reshape(DEPTH, Q_LORA, MLA_HEADS, QK_NOPE + QK_ROPE),
                       ((0, 0), (0, 0), (0, 0), (0, HEAD_SLOT - QK_NOPE - QK_ROPE))
                       ).reshape(DEPTH, Q_LORA, MLA_HEADS * HEAD_SLOT).astype(BF16)
    w_ukv_bf = w_ukv.astype(BF16)
    w_sgu_bf = w_sgu.astype(BF16)
    w_br = [w.astype(BF16) for w in (w_br_mla, w_br_sgu, w_br_ret)]
    w_out_bf = w_out.astype(BF16)
    b_sgu_exp = jnp.repeat(b_sgu, LANE, axis=2)
    decay_b = jnp.broadcast_to(ret_decay[:, :, :, None, None], (DEPTH, 2, RET_HEADS, 8, LANE))
    rope_tabs = _rope_tables()
    kc, vc = _ctxkv(cache_ckv, cache_krope, w_ukv_bf)

    xs = (x_prompt.reshape(N_P, D_MODEL), x_sample.reshape(N_S, D_MODEL))
    h = _prenorm(*xs, g_pre[0][None, :], mod3)
    cache_stacks = ret_stack = None
    for l in range(DEPTH):
        z = _inproj(h, w_in_t, l)
        q_p, k_p, v_p, *cache_stacks = _qkv(z, h, w_kr, w_uq_pad, w_ukv_bf, g_q[l][None, :], g_kv[l][None, :], None,
                                            l, 0, cache_stacks)
        a_mla_p = _attn_prompt(q_p, k_p, v_p, z)
        q_s, k_s, v_s = _qkv(z, h, w_kr, w_uq_pad, w_ukv_bf, g_q[l][None, :], g_kv[l][None, :], rope_tabs, l, 1)
        a_mla_s = _attn_sample(q_s, k_s, v_s, kc, vc, z, l)
        a_sgu = _sgu(z, g_sgu[l][None, :], w_sgu_bf, b_sgu_exp, l)
        a_ret_p, ret_stack = _retention(z, decay_b, g_ret[l][None, :], None, l, 0, ret_stack)
        (a_ret_s,) = _retention(z, decay_b, g_ret[l][None, :], state_ret, l, 1)
        g_pre_next = g_pre[l + 1][None, :] if l + 1 < DEPTH else None
        outs = _tail(a_mla_p, a_mla_s, a_sgu, a_ret_p, a_ret_s, z, *w_br, w_out_bf, xs, g_post[l][None, :],
                     g_pre_next, mod3, l)
        if l + 1 < DEPTH:
            x, h = outs
            xs = (x,)
        else:
            y_p, y_s = outs
    return (y_p.reshape(BATCH, SEQ, D_MODEL), y_s.reshape(DEC_BATCH, DEC_SEQ, D_MODEL), *cache_stacks, ret_stack)
```

```python
import functools

import jax
import jax.numpy as jnp
import numpy as np
from jax import lax
from jax.experimental import pallas as pl
from jax.experimental.pallas import tpu as pltpu

F32 = jnp.float32
BF16 = jnp.bfloat16

D_MODEL = 2048
BATCH = 16
SEQ = 256
DEPTH = 4
DEC_BATCH = 2
DEC_SEQ = 2048
PAST_LEN = 256
GRID_W = 64
EPS = 1e-6
ROPE_BASE = 10000.0

MLA_HEADS = 8
QK_NOPE = 128
QK_ROPE = 64
V_HEAD = 128
Q_LORA = 512
KV_LORA = 512
MLA_W = MLA_HEADS * V_HEAD
SGU_CHUNK = 128
SGU_GROUPS = 8
SGU_W = 1024
RET_HEADS = 8
RET_DK = 128
RET_DV = 128
RET_W = RET_HEADS * RET_DV
RET_CHUNK = 128
N_BRANCH = 3

LANE = 128
N_P = BATCH * SEQ
N_S = DEC_BATCH * DEC_SEQ
N_TOK = N_P + N_S
COND_ROWS = 8
HEAD_SLOT = 2 * LANE
KR_COL = Q_LORA + KV_LORA
Z_BLK = 1024
N_ZBLK = 15
ZB_GP_MLA, ZB_U, ZB_VS, ZB_GP_SGU, ZB_RQ, ZB_RK, ZB_RV, ZB_GP_RET, ZB_MERGE = 1, 2, 3, 4, 5, 6, 7, 8, 9
VMEM_LIMIT = 56 * 1024 * 1024
LOG2E = 1.4426950408889634

_NT = (((1,), (1,)), ((), ()))


def _params(n_axes, vmem=VMEM_LIMIT):
    return pltpu.CompilerParams(dimension_semantics=("arbitrary",) * n_axes, vmem_limit_bytes=vmem)


def _rms(x, g):
    return x * lax.rsqrt(jnp.mean(x * x, axis=-1, keepdims=True) + EPS) * g


def _silu(x):
    return x * jax.nn.sigmoid(x)


def _cond_row(tile, tm):
    return jnp.where(tile < N_P // tm, 0, 1 + (tile - N_P // tm) // (DEC_SEQ // tm))


def _mod_spec(l, tm):
    return pl.BlockSpec((1, 3, D_MODEL), lambda i: (l * COND_ROWS + _cond_row(i, tm), 0, 0))


def _prenorm_value(x, g, mod_ref):
    return (_rms(x, g) * (1.0 + mod_ref[0, 1:2, :]) + mod_ref[0, 0:1, :]).astype(BF16)


def _mod_kernel(cond_ref, w_ref, b_ref, o_ref):
    s = _silu(cond_ref[...]).astype(BF16)
    o_ref[0] = jnp.dot(s, w_ref[0].astype(BF16), preferred_element_type=F32) + b_ref[0]


def _modulation(cond8, w_mod, b_mod):
    tn = 1024
    return pl.pallas_call(
        _mod_kernel,
        grid=(DEPTH, 3 * D_MODEL // tn),
        in_specs=[pl.BlockSpec((COND_ROWS, D_MODEL), lambda l, n: (0, 0)),
                  pl.BlockSpec((1, D_MODEL, tn), lambda l, n: (l, 0, n)),
                  pl.BlockSpec((1, 1, tn), lambda l, n: (l, 0, n))],
        out_specs=pl.BlockSpec((1, COND_ROWS, tn), lambda l, n: (l, 0, n)),
        out_shape=jax.ShapeDtypeStruct((DEPTH, COND_ROWS, 3 * D_MODEL), F32),
        compiler_params=_params(2),
        name="modulation",
    )(cond8, w_mod, b_mod.reshape(DEPTH, 1, 3 * D_MODEL))


def _prenorm_kernel(xp_ref, xs_ref, g_ref, mod_ref, h_ref, *, tiles_p):
    x = jnp.where(pl.program_id(0) < tiles_p, xp_ref[...], xs_ref[...])
    h_ref[...] = _prenorm_value(x, g_ref[...], mod_ref)


def _prenorm(x_p, x_s, g_pre_l, mod3):
    tm = 512
    tiles_p = N_P // tm
    return pl.pallas_call(
        functools.partial(_prenorm_kernel, tiles_p=tiles_p),
        grid=(N_TOK // tm,),
        in_specs=[pl.BlockSpec((tm, D_MODEL), lambda i: (jnp.minimum(i, tiles_p - 1), 0)),
                  pl.BlockSpec((tm, D_MODEL), lambda i: (jnp.maximum(i - tiles_p, 0), 0)),
                  pl.BlockSpec((1, D_MODEL), lambda i: (0, 0)),
                  _mod_spec(0, tm)],
        out_specs=pl.BlockSpec((tm, D_MODEL), lambda i: (i, 0)),
        out_shape=jax.ShapeDtypeStruct((N_TOK, D_MODEL), BF16),
        compiler_params=_params(1),
        name="prenorm",
    )(x_p, x_s, g_pre_l, mod3)


def _inproj_kernel(h_ref, w_ref, o_ref, wbf_ref):
    j = pl.program_id(0)

    @pl.when(pl.program_id(1) == 0)
    def _():
        step = 128
        for r in range(0, Z_BLK, step):
            wbf_ref[r:r + step, :] = w_ref[0, r:r + step, :].astype(BF16)

    is_silu = jnp.logical_or(jnp.logical_or(j == ZB_GP_MLA, j == ZB_GP_SGU), j == ZB_GP_RET)
    acc = lax.dot_general(h_ref[...], wbf_ref[...], _NT, preferred_element_type=F32)
    sig = 0.5 * jnp.tanh(0.5 * acc) + 0.5
    o_ref[...] = jnp.where(is_silu, acc * sig, jnp.where(j >= ZB_MERGE, sig, acc))


def _inproj(h, w_in_t, l):
    tm = 1024
    return pl.pallas_call(
        _inproj_kernel,
        grid=(N_ZBLK, N_TOK // tm),
        in_specs=[pl.BlockSpec((tm, D_MODEL), lambda j, i: (i, 0)),
                  pl.BlockSpec((pl.Element(1), pl.Element(Z_BLK), pl.Element(D_MODEL)),
                               lambda j, i: (l, pl.multiple_of(Z_BLK * j + jnp.where(j > 0, QK_ROPE, 0), QK_ROPE),
                                             0))],
        out_specs=pl.BlockSpec((tm, Z_BLK), lambda j, i: (i, j)),
        out_shape=jax.ShapeDtypeStruct((N_TOK, N_ZBLK * Z_BLK), F32),
        scratch_shapes=[pltpu.VMEM((Z_BLK, D_MODEL), BF16)],
        compiler_params=_params(2),
        name="inproj",
    )(h, w_in_t)


def _rope(x, c, s1, s2):
    return x * c + pltpu.roll(x, LANE - QK_ROPE // 4, 1) * s1 + pltpu.roll(x, QK_ROPE // 4, 1) * s2


def _qkv_kernel(z_ref, h_ref, wkr_ref, wuq_ref, wukv_ref, gq_ref, gkv_ref, *rest, rope):
    if rope:
        c_ref, s1_ref, s2_ref, q_ref, k_ref, v_ref = rest
    else:
        q_ref, k_ref, v_ref, ckvn_ref, kr_ref = rest[-5:]
    cq = z_ref[:, :Q_LORA]
    ckv = z_ref[:, Q_LORA:]
    q = jnp.dot(_rms(cq, gq_ref[...]).astype(BF16), wuq_ref[0], preferred_element_type=F32) * _EXP2_SCALE
    ckvn = _rms(ckv, gkv_ref[...])
    kv = jnp.dot(ckvn.astype(BF16), wukv_ref[0], preferred_element_type=F32)
    kr = lax.dot_general(h_ref[...], wkr_ref[0], _NT, preferred_element_type=F32)
    if rope:
        c, s1, s2 = c_ref[...], s1_ref[...], s2_ref[...]
        kr = _rope(kr, c, s1, s2)
    else:
        for b in range(ckvn_ref.shape[0]):
            rows = slice(b * SEQ, (b + 1) * SEQ)
            for s in range(ckvn_ref.shape[1]):
                ckvn_ref[b, s] = ckvn[rows, :] if s == 0 else jnp.zeros((SEQ, KV_LORA), F32)
                kr_ref[b, s] = kr[rows, :QK_ROPE] if s == 0 else jnp.zeros((SEQ, QK_ROPE), F32)
    kr_bf = kr.astype(BF16)
    for hd in range(MLA_HEADS):
        lo = hd * HEAD_SLOT
        q_ref[:, lo:lo + LANE] = q[:, lo:lo + LANE].astype(BF16)
        qr = q[:, lo + LANE:lo + HEAD_SLOT]
        if rope:
            qr = _rope(qr, c, s1, s2)
        q_ref[:, lo + LANE:lo + HEAD_SLOT] = qr.astype(BF16)
        k_ref[:, lo:lo + LANE] = kv[:, lo:lo + LANE].astype(BF16)
        k_ref[:, lo + LANE:lo + HEAD_SLOT] = kr_bf
        v_ref[:, hd * LANE:(hd + 1) * LANE] = kv[:, lo + LANE:lo + HEAD_SLOT].astype(BF16)


def _qkv(z, h, w_kr, w_uq_pad, w_ukv, g_q_l, g_kv_l, rope_tabs, l, group, stacks=None):
    tm = 512
    aliases = {}
    n_tok = N_P if group == 0 else N_S
    t0 = 0 if group == 0 else N_P // tm
    rope = group == 1
    in_specs = [pl.BlockSpec((tm, Z_BLK), lambda i: (i + t0, 0)),
                pl.BlockSpec((tm, D_MODEL), lambda i: (i + t0, 0)),
                pl.BlockSpec((1, LANE, D_MODEL), lambda i: (l, 0, 0)),
                pl.BlockSpec((1, Q_LORA, MLA_HEADS * HEAD_SLOT), lambda i: (l, 0, 0)),
                pl.BlockSpec((1, KV_LORA, MLA_HEADS * HEAD_SLOT), lambda i: (l, 0, 0)),
                pl.BlockSpec((1, Q_LORA), lambda i: (0, 0)),
                pl.BlockSpec((1, KV_LORA), lambda i: (0, 0))]
    args = [z, h, w_kr, w_uq_pad, w_ukv, g_q_l, g_kv_l]
    out_specs = [pl.BlockSpec((tm, MLA_HEADS * HEAD_SLOT), lambda i: (i, 0)),
                 pl.BlockSpec((tm, MLA_HEADS * HEAD_SLOT), lambda i: (i, 0)),
                 pl.BlockSpec((tm, MLA_W), lambda i: (i, 0))]
    out_shape = [jax.ShapeDtypeStruct((n_tok, MLA_HEADS * HEAD_SLOT), BF16),
                 jax.ShapeDtypeStruct((n_tok, MLA_HEADS * HEAD_SLOT), BF16),
                 jax.ShapeDtypeStruct((n_tok, MLA_W), BF16)]
    if rope:
        tiles_per_seq = DEC_SEQ // tm
        in_specs += [pl.BlockSpec((tm, LANE), lambda i: (i % tiles_per_seq, 0))] * 3
        args += list(rope_tabs)
    else:
        assert (l == 0) == (stacks is None)
        slices = DEPTH if stacks is None else 1
        out_specs += [pl.BlockSpec((tm // SEQ, slices, SEQ, KV_LORA), lambda i: (i, l, 0, 0)),
                      pl.BlockSpec((tm // SEQ, slices, SEQ, QK_ROPE), lambda i: (i, l, 0, 0))]
        out_shape += [jax.ShapeDtypeStruct((BATCH, DEPTH, SEQ, KV_LORA), F32),
                      jax.ShapeDtypeStruct((BATCH, DEPTH, SEQ, QK_ROPE), F32)]
        if stacks is not None:
            aliases = {len(args): 3, len(args) + 1: 4}
            in_specs += [pl.BlockSpec(memory_space=pl.ANY)] * 2
            args += list(stacks)
    return pl.pallas_call(
        functools.partial(_qkv_kernel, rope=rope),
        grid=(n_tok // tm,),
        in_specs=in_specs,
        out_specs=out_specs,
        out_shape=out_shape,
        input_output_aliases=aliases,
        compiler_params=_params(1),
        name="qkv_sample" if rope else "qkv_prompt",
    )(*args)


def _ctxkv_kernel(ckv_ref, kr_ref, w_ref, k_ref, v_ref):
    kv = jnp.dot(ckv_ref[0, 0].astype(BF16), w_ref[0], preferred_element_type=F32)
    kr = jnp.concatenate([kr_ref[0, 0], jnp.zeros((PAST_LEN, LANE - QK_ROPE), F32)], axis=1).astype(BF16)
    for hd in range(MLA_HEADS):
        lo = hd * HEAD_SLOT
        k_ref[0, 0, :, lo:lo + LANE] = kv[:, lo:lo + LANE].astype(BF16)
        k_ref[0, 0, :, lo + LANE:lo + HEAD_SLOT] = kr
        v_ref[0, 0, :, hd * LANE:(hd + 1) * LANE] = kv[:, lo + LANE:lo + HEAD_SLOT].astype(BF16)


def _ctxkv(cache_ckv, cache_krope, w_ukv):
    return pl.pallas_call(
        _ctxkv_kernel,
        grid=(DEPTH, DEC_BATCH),
        in_specs=[pl.BlockSpec((1, 1, PAST_LEN, KV_LORA), lambda l, b: (b, l, 0, 0)),
                  pl.BlockSpec((1, 1, PAST_LEN, QK_ROPE), lambda l, b: (b, l, 0, 0)),
                  pl.BlockSpec((1, KV_LORA, MLA_HEADS * HEAD_SLOT), lambda l, b: (l, 0, 0))],
        out_specs=[pl.BlockSpec((1, 1, PAST_LEN, MLA_HEADS * HEAD_SLOT), lambda l, b: (l, b, 0, 0)),
                   pl.BlockSpec((1, 1, PAST_LEN, MLA_W), lambda l, b: (l, b, 0, 0))],
        out_shape=[jax.ShapeDtypeStruct((DEPTH, DEC_BATCH, PAST_LEN, MLA_HEADS * HEAD_SLOT), BF16),
                   jax.ShapeDtypeStruct((DEPTH, DEC_BATCH, PAST_LEN, MLA_W), BF16)],
        compiler_params=_params(2),
        name="ctxkv",
    )(cache_ckv, cache_krope, w_ukv)


_EXP2_SCALE = (QK_NOPE + QK_ROPE) ** -0.5 * LOG2E


def _attn_prompt_kernel(q_ref, k_ref, v_ref, gp_ref, o_ref):
    for b in range(q_ref.shape[0] // SEQ):
        rows = slice(b * SEQ, (b + 1) * SEQ)
        for hd in range(MLA_HEADS):
            ks = slice(hd * HEAD_SLOT, (hd + 1) * HEAD_SLOT)
            vs = slice(hd * LANE, (hd + 1) * LANE)
            s = lax.dot_general(q_ref[rows, ks], k_ref[rows, ks], _NT, preferred_element_type=F32)
            e = jnp.exp2(s - jnp.max(s, axis=-1, keepdims=True))
            p = (e * (1.0 / jnp.sum(e, axis=-1, keepdims=True))).astype(BF16)
            o = jnp.dot(p, v_ref[rows, vs], preferred_element_type=F32)
            o_ref[rows, vs] = (gp_ref[rows, vs] * o).astype(BF16)


def _attn_prompt(q, k, v, z):
    tm = 2 * SEQ
    return pl.pallas_call(
        _attn_prompt_kernel,
        grid=(N_P // tm,),
        in_specs=[pl.BlockSpec((tm, MLA_HEADS * HEAD_SLOT), lambda i: (i, 0)),
                  pl.BlockSpec((tm, MLA_HEADS * HEAD_SLOT), lambda i: (i, 0)),
                  pl.BlockSpec((tm, MLA_W), lambda i: (i, 0)),
                  pl.BlockSpec((tm, Z_BLK), lambda i: (i, ZB_GP_MLA))],
        out_specs=pl.BlockSpec((tm, MLA_W), lambda i: (i, 0)),
        out_shape=jax.ShapeDtypeStruct((N_P, MLA_W), BF16),
        compiler_params=_params(1),
        name="attn_prompt",
    )(q, k, v, z)


def _attn_sample_kernel(q_ref, k_ref, v_ref, kc_ref, vc_ref, gp_ref, o_ref, *, kchunk, heads):
    def lane_fold(x):
        out = x[:, :LANE]
        for c0 in range(LANE, x.shape[1], LANE):
            out = out + x[:, c0:c0 + LANE]
        return out

    for hd in range(heads):
        ks = slice(hd * HEAD_SLOT, (hd + 1) * HEAD_SLOT)
        vs = slice(hd * LANE, (hd + 1) * LANE)
        q = q_ref[:, ks]
        chunks = [(kc_ref[0, 0, :, ks], vc_ref[0, 0, :, vs])]
        chunks += [(k_ref[r:r + kchunk, ks], v_ref[r:r + kchunk, vs]) for r in range(0, DEC_SEQ, kchunk)]
        m = l = acc = None
        for k, v in chunks:
            s = lax.dot_general(q, k, _NT, preferred_element_type=F32)
            m_chunk = jnp.max(s, axis=-1, keepdims=True)
            if m is None:
                m = m_chunk
                e = jnp.exp2(s - m)
                l = lane_fold(e)
                acc = jnp.dot(e.astype(BF16), v, preferred_element_type=F32)
            else:
                m_new = jnp.maximum(m, m_chunk)
                alpha = jnp.exp2(m - m_new)
                e = jnp.exp2(s - m_new)
                l = alpha * l + lane_fold(e)
                acc = alpha * acc + jnp.dot(e.astype(BF16), v, preferred_element_type=F32)
                m = m_new
        inv = 1.0 / jnp.sum(l, axis=-1, keepdims=True)
        o_ref[:, vs] = (gp_ref[:, vs] * (acc * inv)).astype(BF16)


def _attn_sample(q, k, v, kc, vc, z, l):
    tq = 512
    kchunk = PAST_LEN
    heads = 4
    tiles_per_seq = DEC_SEQ // tq
    z_tile0 = N_P // tq
    gp_blk0 = ZB_GP_MLA * Z_BLK // (heads * LANE)
    return pl.pallas_call(
        functools.partial(_attn_sample_kernel, kchunk=kchunk, heads=heads),
        grid=(DEC_BATCH, MLA_HEADS // heads, tiles_per_seq),
        in_specs=[pl.BlockSpec((tq, heads * HEAD_SLOT), lambda b, hb, t: (b * tiles_per_seq + t, hb)),
                  pl.BlockSpec((DEC_SEQ, heads * HEAD_SLOT), lambda b, hb, t: (b, hb)),
                  pl.BlockSpec((DEC_SEQ, heads * LANE), lambda b, hb, t: (b, hb)),
                  pl.BlockSpec((1, 1, PAST_LEN, heads * HEAD_SLOT), lambda b, hb, t: (l, b, 0, hb)),
                  pl.BlockSpec((1, 1, PAST_LEN, heads * LANE), lambda b, hb, t: (l, b, 0, hb)),
                  pl.BlockSpec((tq, heads * LANE),
                               lambda b, hb, t: (z_tile0 + b * tiles_per_seq + t, gp_blk0 + hb))],
        out_specs=pl.BlockSpec((tq, heads * LANE), lambda b, hb, t: (b * tiles_per_seq + t, hb)),
        out_shape=jax.ShapeDtypeStruct((N_S, MLA_W), BF16),
        compiler_params=_params(3),
        name="attn_sample",
    )(q, k, v, kc, vc, z)


def _sgu_kernel(u_ref, v_ref, gp_ref, g_ref, w_ref, b_ref, o_ref, *, chunks):
    for c in range(chunks):
        rows = slice(c * SGU_CHUNK, (c + 1) * SGU_CHUNK)
        vn = _rms(v_ref[rows, :], g_ref[...]).astype(BF16)
        for g in range(SGU_GROUPS):
            cols = slice(g * LANE, (g + 1) * LANE)
            s = jnp.dot(w_ref[0, g], vn[:, cols], preferred_element_type=F32) + b_ref[0, :, cols]
            o_ref[rows, cols] = (gp_ref[rows, cols] * (u_ref[rows, cols] * s)).astype(BF16)


def _sgu(z, g_sgu_l, w_sgu, b_sgu_exp, l):
    tm = 1024
    return pl.pallas_call(
        functools.partial(_sgu_kernel, chunks=tm // SGU_CHUNK),
        grid=(N_TOK // tm,),
        in_specs=[pl.BlockSpec((tm, Z_BLK), lambda i: (i, ZB_U)),
                  pl.BlockSpec((tm, Z_BLK), lambda i: (i, ZB_VS)),
                  pl.BlockSpec((tm, Z_BLK), lambda i: (i, ZB_GP_SGU)),
                  pl.BlockSpec((1, SGU_W), lambda i: (0, 0)),
                  pl.BlockSpec((1, SGU_GROUPS, SGU_CHUNK, SGU_CHUNK), lambda i: (l, 0, 0, 0)),
                  pl.BlockSpec((1, SGU_CHUNK, SGU_W), lambda i: (l, 0, 0))],
        out_specs=pl.BlockSpec((tm, SGU_W), lambda i: (i, 0)),
        out_shape=jax.ShapeDtypeStruct((N_TOK, SGU_W), BF16),
        compiler_params=_params(1),
        name="sgu",
    )(z, z, z, g_sgu_l, w_sgu, b_sgu_exp)


def _log_sigmoid(x):
    return -(jnp.maximum(-x, 0.0) + jnp.log1p(jnp.exp(-jnp.abs(x))))


def _ret_kernel(q_ref, k_ref, v_ref, gp_ref, dec_ref, g_ref, *rest, block, n_chunks, heads, has_state,
                emit_state):
    rest = list(rest)
    r0_ref = rest.pop(0) if has_state else None
    rout_ref = rest.pop() if emit_state else None
    o_ref = rest.pop()
    C = block
    shape = (RET_DK, RET_DV)
    ii = lax.broadcasted_iota(jnp.int32, (C, C), 0).astype(F32)
    jj = lax.broadcasted_iota(jnp.int32, (C, C), 1).astype(F32)
    ri = lax.broadcasted_iota(jnp.int32, (C, RET_DV), 0).astype(F32)
    cj = lax.broadcasted_iota(jnp.int32, (RET_DK, C), 1).astype(F32)
    lower = ii >= jj
    k_scale = RET_DK ** -0.5

    for hd in range(heads):
        cols = slice(hd * LANE, (hd + 1) * LANE)
        lg_f8 = _log_sigmoid(dec_ref[0, 0, hd])
        lg_b8 = _log_sigmoid(dec_ref[0, 1, hd])

        def spread(x8, rows, width):
            return jnp.tile(x8, (rows // 8, width // LANE))

        dec = jnp.where(lower, jnp.exp(jnp.where(lower, ii - jj, 0.0) * spread(lg_f8, C, C)),
                        jnp.exp(jnp.where(lower, 0.0, jj - ii) * spread(lg_b8, C, C)))
        xi_f = jnp.exp((ri + 1.0) * spread(lg_f8, C, RET_DV))
        xi_b = jnp.exp(((C - 1.0 - ri) + 1.0) * spread(lg_b8, C, RET_DV))
        zeta_f = jnp.exp((C - 1.0 - cj) * spread(lg_f8, RET_DK, C))
        zeta_b = jnp.exp((C - 1.0 - (C - 1.0 - cj)) * spread(lg_b8, RET_DK, C))
        gc_f = jnp.exp(C * spread(lg_f8, RET_DK, RET_DV))
        gc_b = jnp.exp(C * spread(lg_b8, RET_DK, RET_DV))

        def rows_of(c):
            return slice(c * C, (c + 1) * C)

        kvs = []
        for c in range(n_chunks):
            k_t = jnp.transpose(k_ref[rows_of(c), cols] * k_scale)
            kz = jnp.concatenate([k_t * zeta_f, k_t * zeta_b], axis=0).astype(BF16)
            kvs.append(jnp.dot(kz, v_ref[rows_of(c), cols].astype(BF16), preferred_element_type=F32))

        rfs, rbs = [None] * n_chunks, [None] * n_chunks
        r_f = r0_ref[0, 0, 0, hd] if has_state else jnp.zeros(shape, F32)
        for c in range(n_chunks):
            rfs[c] = r_f.astype(BF16)
            r_f = gc_f * r_f + kvs[c][:RET_DK, :]
        r_b = r0_ref[0, 0, 1, hd] if has_state else jnp.zeros(shape, F32)
        for c in reversed(range(n_chunks)):
            rbs[c] = r_b.astype(BF16)
            r_b = gc_b * r_b + kvs[c][RET_DK:, :]
        if emit_state:
            for s in range(rout_ref.shape[1]):
                rout_ref[0, s, 0, hd] = r_f if s == 0 else jnp.zeros(shape, F32)
                rout_ref[0, s, 1, hd] = r_b if s == 0 else jnp.zeros(shape, F32)

        for c in range(n_chunks):
            q = q_ref[rows_of(c), cols].astype(BF16)
            k = (k_ref[rows_of(c), cols] * k_scale).astype(BF16)
            att = lax.dot_general(q, k, _NT, preferred_element_type=F32) * dec
            o = jnp.dot(att.astype(BF16), v_ref[rows_of(c), cols].astype(BF16), preferred_element_type=F32)
            use_f = has_state or c > 0
            use_b = has_state or c < n_chunks - 1
            if use_f and use_b:
                inter = jnp.dot(q, jnp.concatenate([rfs[c], rbs[c]], axis=1), preferred_element_type=F32)
                o = o + inter[:, :RET_DV] * xi_f + inter[:, RET_DV:] * xi_b
            elif use_f:
                o = o + jnp.dot(q, rfs[c], preferred_element_type=F32) * xi_f
            elif use_b:
                o = o + jnp.dot(q, rbs[c], preferred_element_type=F32) * xi_b
            mu = jnp.mean(o, axis=-1, keepdims=True)
            var = jnp.mean(jnp.square(o - mu), axis=-1, keepdims=True)
            y = (o - mu) * lax.rsqrt(var + EPS) * g_ref[:, cols]
            o_ref[rows_of(c), cols] = (gp_ref[rows_of(c), cols] * y).astype(BF16)


def _retention(z, decay_b, g_ret_l, state_ret, l, group, stack=None):
    seq = SEQ if group == 0 else DEC_SEQ
    nb = BATCH if group == 0 else DEC_BATCH
    heads = RET_HEADS if group == 0 else 2
    hblk = RET_HEADS // heads
    b0 = 0 if group == 0 else N_P // seq
    has_state = group == 1
    emit_state = group == 0
    block = 2 * RET_CHUNK
    n_chunks = seq // block
    width = heads * LANE

    def zspec(zb):
        return pl.BlockSpec((seq, width), lambda b, hb: (b0 + b, zb * Z_BLK // width + hb))

    in_specs = [zspec(ZB_RQ), zspec(ZB_RK), zspec(ZB_RV), zspec(ZB_GP_RET),
                pl.BlockSpec((1, 2, heads, 8, LANE), lambda b, hb: (l, 0, hb, 0, 0)),
                pl.BlockSpec((1, width), lambda b, hb: (0, hb))]
    args = [z, z, z, z, decay_b, g_ret_l]
    state_block = (1, 1, 2, heads, RET_DK, RET_DV)
    if has_state:
        in_specs.append(pl.BlockSpec(state_block, lambda b, hb: (b, l, 0, hb, 0, 0)))
        args.append(state_ret)
    out_specs = [pl.BlockSpec((seq, width), lambda b, hb: (b, hb))]
    out_shape = [jax.ShapeDtypeStruct((nb * seq, RET_W), BF16)]
    aliases = {}
    if emit_state:
        assert (l == 0) == (stack is None)
        slices = DEPTH if stack is None else 1
        out_specs.append(pl.BlockSpec((1, slices) + state_block[2:], lambda b, hb: (b, l, 0, hb, 0, 0)))
        out_shape.append(jax.ShapeDtypeStruct((nb, DEPTH, 2, RET_HEADS, RET_DK, RET_DV), F32))
        if stack is not None:
            aliases = {len(args): 1}
            in_specs.append(pl.BlockSpec(memory_space=pl.ANY))
            args.append(stack)
    return pl.pallas_call(
        functools.partial(_ret_kernel, block=block, n_chunks=n_chunks, heads=heads, has_state=has_state,
                          emit_state=emit_state),
        grid=(nb, hblk),
        in_specs=in_specs,
        out_specs=out_specs,
        out_shape=out_shape,
        input_output_aliases=aliases,
        compiler_params=_params(2),
        name="ret_sample" if has_state else "ret_prompt",
    )(*args)


def _tail_kernel(ap_m, as_m, a_sgu, ap_r, as_r, m_ref, wm_ref, ws_ref, wr_ref, wo_ref,
                 gpost_ref, mod_ref, *rest, tiles_p, first, last):
    i = pl.program_id(0)
    is_p = i < tiles_p
    rest = list(rest)
    if first:
        x = jnp.where(is_p, rest.pop(0)[...], rest.pop(0)[...])
    else:
        x = rest.pop(0)[...]
    a_mla = jnp.where(is_p, ap_m[...], as_m[...])
    a_ret = jnp.where(is_p, ap_r[...], as_r[...])
    y_mla = jnp.dot(a_mla, wm_ref[0], preferred_element_type=F32)
    y = m_ref[:, :D_MODEL] * y_mla
    y_sgu = jnp.dot(a_sgu[...], ws_ref[0], preferred_element_type=F32)
    y = y + m_ref[:, D_MODEL:2 * D_MODEL] * y_sgu
    y_ret = jnp.dot(a_ret, wr_ref[0], preferred_element_type=F32)
    y = y + m_ref[:, 2 * D_MODEL:] * y_ret
    out = jnp.dot(y.astype(BF16), wo_ref[0], preferred_element_type=F32)
    x_new = x + mod_ref[0, 2:3, :] * _rms(out, gpost_ref[...])
    if last:
        yp_ref, ys_ref = rest

        @pl.when(is_p)
        def _():
            yp_ref[...] = x_new

        @pl.when(jnp.logical_not(is_p))
        def _():
            ys_ref[...] = x_new
    else:
        gpre_ref, modn_ref, x_out, h_out = rest
        x_out[...] = x_new
        h_out[...] = _prenorm_value(x_new, gpre_ref[...], modn_ref)


def _tail(a_mla_p, a_mla_s, a_sgu, a_ret_p, a_ret_s, z, w_br_mla, w_br_sgu, w_br_ret, w_out, xs, g_post_l,
          g_pre_next, mod3, l):
    tm = 256
    tiles_p = N_P // tm
    last = g_pre_next is None
    first = len(xs) == 2

    def p_spec():
        return pl.BlockSpec((tm, 1024), lambda i: (jnp.minimum(i, tiles_p - 1), 0))

    def s_spec():
        return pl.BlockSpec((tm, 1024), lambda i: (jnp.maximum(i - tiles_p, 0), 0))

    def w_spec(rows):
        return pl.BlockSpec((1, rows, D_MODEL), lambda i: (l, 0, 0), pipeline_mode=pl.Buffered(1))

    def row_spec():
        return pl.BlockSpec((tm, D_MODEL), lambda i: (i, 0))

    in_specs = [p_spec(), s_spec(), pl.BlockSpec((tm, 1024), lambda i: (i, 0)), p_spec(), s_spec()]
    in_specs += [pl.BlockSpec((pl.Element(tm), pl.Element(N_BRANCH * D_MODEL)),
                              lambda i: (pl.multiple_of(i * tm, tm), ZB_MERGE * Z_BLK))]
    in_specs += [w_spec(1024), w_spec(1024), w_spec(1024), w_spec(D_MODEL),
                 pl.BlockSpec((1, D_MODEL), lambda i: (0, 0)), _mod_spec(l, tm)]
    args = [a_mla_p, a_mla_s, a_sgu, a_ret_p, a_ret_s, z]
    args += [w_br_mla, w_br_sgu, w_br_ret, w_out, g_post_l, mod3]
    if first:
        in_specs += [pl.BlockSpec((tm, D_MODEL), lambda i: (jnp.minimum(i, tiles_p - 1), 0)),
                     pl.BlockSpec((tm, D_MODEL), lambda i: (jnp.maximum(i - tiles_p, 0), 0))]
    else:
        in_specs += [row_spec()]
    args += list(xs)
    if last:
        out_specs = [pl.BlockSpec((tm, D_MODEL), lambda i: (jnp.minimum(i, tiles_p - 1), 0)),
                     pl.BlockSpec((tm, D_MODEL), lambda i: (jnp.maximum(i - tiles_p, 0), 0))]
        out_shape = [jax.ShapeDtypeStruct((N_P, D_MODEL), F32), jax.ShapeDtypeStruct((N_S, D_MODEL), F32)]
    else:
        in_specs += [pl.BlockSpec((1, D_MODEL), lambda i: (0, 0)), _mod_spec(l + 1, tm)]
        args += [g_pre_next, mod3]
        out_specs = [row_spec(), row_spec()]
        out_shape = [jax.ShapeDtypeStruct((N_TOK, D_MODEL), F32), jax.ShapeDtypeStruct((N_TOK, D_MODEL), BF16)]
    return pl.pallas_call(
        functools.partial(_tail_kernel, tiles_p=tiles_p, first=first, last=last),
        grid=(N_TOK // tm,),
        in_specs=in_specs,
        out_specs=out_specs,
        out_shape=out_shape,
        compiler_params=_params(1),
        name="tail_last" if last else "tail",
    )(*args)


def _rope_tables():
    f32 = np.float32
    pos = np.arange(DEC_SEQ)
    row = (pos // GRID_W).astype(f32)
    col = (pos % GRID_W).astype(f32)
    n_freq = QK_ROPE // 4
    inv = f32(ROPE_BASE) ** (-np.arange(n_freq, dtype=f32) / f32(n_freq))
    ang_r = row[:, None] * inv
    ang_c = col[:, None] * inv
    zeros = np.zeros((DEC_SEQ, n_freq), f32)
    pad = np.zeros((DEC_SEQ, LANE - QK_ROPE), f32)
    cr, sr, cc, sc = np.cos(ang_r), np.sin(ang_r), np.cos(ang_c), np.sin(ang_c)
    c = np.concatenate([cr, cr, cc, cc, pad], axis=1)
    s1 = np.concatenate([-sr, zeros, -sc, zeros, pad], axis=1)
    s2 = np.concatenate([zeros, sr, zeros, sc, pad], axis=1)
    return tuple(jnp.asarray(t.astype(f32)) for t in (c, s1, s2))


def kernel(x_prompt, x_sample, cache_ckv, cache_krope, state_ret, c, c_ctx, w_mod, b_mod, g_pre, g_post, w_in,
           g_q, g_kv, w_uq, w_ukv, g_sgu, w_sgu, b_sgu, ret_decay, g_ret, w_br_mla, w_br_sgu, w_br_ret, w_out):
    cond8 = jnp.concatenate([c_ctx[None, :], c, jnp.zeros((COND_ROWS - 1 - DEC_BATCH, D_MODEL), F32)], axis=0)
    mod3 = _modulation(cond8, w_mod, b_mod).reshape(DEPTH * COND_ROWS, 3, D_MODEL)
    w_in_t = jnp.swapaxes(w_in, 1, 2)
    w_kr = jnp.pad(w_in_t[:, KR_COL:KR_COL + QK_ROPE, :], ((0, 0), (0, LANE - QK_ROPE), (0, 0))).astype(BF16)
    w_uq_pad = jnp.pad(w_uq.reshape(DEPTH, Q_LORA, MLA_HEADS, QK_NOPE + QK_ROPE),
                       ((0, 0), (0, 0), (0, 0), (0, HEAD_SLOT - QK_NOPE - QK_ROPE))
                       ).reshape(DEPTH, Q_LORA, MLA_HEADS * HEAD_SLOT).astype(BF16)
    w_ukv_bf = w_ukv.astype(BF16)
    w_sgu_bf = w_sgu.astype(BF16)
    w_br = [w.astype(BF16) for w in (w_br_mla, w_br_sgu, w_br_ret)]
    w_out_bf = w_out.astype(BF16)
    b_sgu_exp = jnp.repeat(b_sgu, LANE, axis=2)
    decay_b = jnp.broadcast_to(ret_decay[:, :, :, None, None], (DEPTH, 2, RET_HEADS, 8, LANE))
    rope_tabs = _rope_tables()
    kc, vc = _ctxkv(cache_ckv, cache_krope, w_ukv_bf)

    xs = (x_prompt.reshape(N_P, D_MODEL), x_sample.reshape(N_S, D_MODEL))
    h = _prenorm(*xs, g_pre[0][None, :], mod3)
    cache_stacks = ret_stack = None
    for l in range(DEPTH):
        z = _inproj(h, w_in_t, l)
        q_p, k_p, v_p, *cache_stacks = _qkv(z, h, w_kr, w_uq_pad, w_ukv_bf, g_q[l][None, :], g_kv[l][None, :], None,
                                            l, 0, cache_stacks)
        a_mla_p = _attn_prompt(q_p, k_p, v_p, z)
        q_s, k_s, v_s = _qkv(z, h, w_kr, w_uq_pad, w_ukv_bf, g_q[l][None, :], g_kv[l][None, :], rope_tabs, l, 1)
        a_mla_s = _attn_sample(q_s, k_s, v_s, kc, vc, z, l)
        a_sgu = _sgu(z, g_sgu[l][None, :], w_sgu_bf, b_sgu_exp, l)
        a_ret_p, ret_stack = _retention(z, decay_b, g_ret[l][None, :], None, l, 0, ret_stack)
        (a_ret_s,) = _retention(z, decay_b, g_ret[l][None, :], state_ret, l, 1)
        g_pre_next = g_pre[l + 1][None, :] if l + 1 < DEPTH else None
        outs = _tail(a_mla_p, a_mla_s, a_sgu, a_ret_p, a_ret_s, z, *w_br, w_out_bf, xs, g_post[l][None, :],
                     g_pre_next, mod3, l)
        if l + 1 < DEPTH:
            x, h = outs
            xs = (x,)
        else:
            y_p, y_s = outs
    return (y_p.reshape(BATCH, SEQ, D_MODEL), y_s.reshape(DEC_BATCH, DEC_SEQ, D_MODEL), *cache_stacks, ret_stack)
```

```python
import functools

import jax
import jax.numpy as jnp
import numpy as np
from jax import lax
from jax.experimental import pallas as pl
from jax.experimental.pallas import tpu as pltpu

F32 = jnp.float32
BF16 = jnp.bfloat16

D_MODEL = 2048
BATCH = 16
SEQ = 256
DEPTH = 4
DEC_BATCH = 2
DEC_SEQ = 2048
PAST_LEN = 256
GRID_W = 64
EPS = 1e-6
ROPE_BASE = 10000.0

MLA_HEADS = 8
QK_NOPE = 128
QK_ROPE = 64
V_HEAD = 128
Q_LORA = 512
KV_LORA = 512
MLA_W = MLA_HEADS * V_HEAD
SGU_CHUNK = 128
SGU_GROUPS = 8
SGU_W = 1024
RET_HEADS = 8
RET_DK = 128
RET_DV = 128
RET_W = RET_HEADS * RET_DV
RET_CHUNK = 128
N_BRANCH = 3

LANE = 128
N_P = BATCH * SEQ
N_S = DEC_BATCH * DEC_SEQ
N_TOK = N_P + N_S
COND_ROWS = 8
HEAD_SLOT = 2 * LANE
KR_COL = Q_LORA + KV_LORA
Z_BLK = 1024
N_ZBLK = 15
ZB_GP_MLA, ZB_U, ZB_VS, ZB_GP_SGU, ZB_RQ, ZB_RK, ZB_RV, ZB_GP_RET, ZB_MERGE = 1, 2, 3, 4, 5, 6, 7, 8, 9
VMEM_LIMIT = 56 * 1024 * 1024
LOG2E = 1.4426950408889634

_NT = (((1,), (1,)), ((), ()))


def _params(n_axes, vmem=VMEM_LIMIT):
    return pltpu.CompilerParams(dimension_semantics=("arbitrary",) * n_axes, vmem_limit_bytes=vmem)


def _rms(x, g):
    return x * lax.rsqrt(jnp.mean(x * x, axis=-1, keepdims=True) + EPS) * g


def _silu(x):
    return x * jax.nn.sigmoid(x)


def _cond_row(tile, tm):
    return jnp.where(tile < N_P // tm, 0, 1 + (tile - N_P // tm) // (DEC_SEQ // tm))


def _mod_spec(l, tm):
    return pl.BlockSpec((1, 3, D_MODEL), lambda i: (l * COND_ROWS + _cond_row(i, tm), 0, 0))


def _prenorm_value(x, g, mod_ref):
    return (_rms(x, g) * (1.0 + mod_ref[0, 1:2, :]) + mod_ref[0, 0:1, :]).astype(BF16)


def _mod_kernel(cond_ref, w_ref, b_ref, o_ref):
    s = _silu(cond_ref[...]).astype(BF16)
    o_ref[0] = jnp.dot(s, w_ref[0].astype(BF16), preferred_element_type=F32) + b_ref[0]


def _modulation(cond8, w_mod, b_mod):
    tn = 1024
    return pl.pallas_call(
        _mod_kernel,
        grid=(DEPTH, 3 * D_MODEL // tn),
        in_specs=[pl.BlockSpec((COND_ROWS, D_MODEL), lambda l, n: (0, 0)),
                  pl.BlockSpec((1, D_MODEL, tn), lambda l, n: (l, 0, n)),
                  pl.BlockSpec((1, 1, tn), lambda l, n: (l, 0, n))],
        out_specs=pl.BlockSpec((1, COND_ROWS, tn), lambda l, n: (l, 0, n)),
        out_shape=jax.ShapeDtypeStruct((DEPTH, COND_ROWS, 3 * D_MODEL), F32),
        compiler_params=_params(2),
        name="modulation",
    )(cond8, w_mod, b_mod.reshape(DEPTH, 1, 3 * D_MODEL))


def _prenorm_kernel(xp_ref, xs_ref, g_ref, mod_ref, h_ref, *, tiles_p):
    x = jnp.where(pl.program_id(0) < tiles_p, xp_ref[...], xs_ref[...])
    h_ref[...] = _prenorm_value(x, g_ref[...], mod_ref)


def _prenorm(x_p, x_s, g_pre_l, mod3):
    tm = 512
    tiles_p = N_P // tm
    return pl.pallas_call(
        functools.partial(_prenorm_kernel, tiles_p=tiles_p),
        grid=(N_TOK // tm,),
        in_specs=[pl.BlockSpec((tm, D_MODEL), lambda i: (jnp.minimum(i, tiles_p - 1), 0)),
                  pl.BlockSpec((tm, D_MODEL), lambda i: (jnp.maximum(i - tiles_p, 0), 0)),
                  pl.BlockSpec((1, D_MODEL), lambda i: (0, 0)),
                  _mod_spec(0, tm)],
        out_specs=pl.BlockSpec((tm, D_MODEL), lambda i: (i, 0)),
        out_shape=jax.ShapeDtypeStruct((N_TOK, D_MODEL), BF16),
        compiler_params=_params(1),
        name="prenorm",
    )(x_p, x_s, g_pre_l, mod3)


def _inproj_kernel(h_ref, w_ref, o_ref, wbf_ref):
    j = pl.program_id(0)

    @pl.when(pl.program_id(1) == 0)
    def _():
        step = 128
        for r in range(0, Z_BLK, step):
            wbf_ref[r:r + step, :] = w_ref[0, r:r + step, :].astype(BF16)

    is_silu = jnp.logical_or(jnp.logical_or(j == ZB_GP_MLA, j == ZB_GP_SGU), j == ZB_GP_RET)
    is_act = jnp.logical_or(is_silu, j >= ZB_MERGE)

    @pl.when(is_act)
    def _():
        acc = lax.dot_general(h_ref[...], wbf_ref[...], _NT, preferred_element_type=F32)
        sig = 0.5 * jnp.tanh(0.5 * acc) + 0.5
        o_ref[...] = jnp.where(is_silu, acc * sig, sig)

    @pl.when(jnp.logical_not(is_act))
    def _():
        o_ref[...] = lax.dot_general(h_ref[...], wbf_ref[...], _NT, preferred_element_type=F32)


def _inproj(h, w_in_t, l):
    tm = 1024
    return pl.pallas_call(
        _inproj_kernel,
        grid=(N_ZBLK, N_TOK // tm),
        in_specs=[pl.BlockSpec((tm, D_MODEL), lambda j, i: (i, 0)),
                  pl.BlockSpec((pl.Element(1), pl.Element(Z_BLK), pl.Element(D_MODEL)),
                               lambda j, i: (l, pl.multiple_of(Z_BLK * j + jnp.where(j > 0, QK_ROPE, 0), QK_ROPE),
                                             0))],
        out_specs=pl.BlockSpec((tm, Z_BLK), lambda j, i: (i, j)),
        out_shape=jax.ShapeDtypeStruct((N_TOK, N_ZBLK * Z_BLK), F32),
        scratch_shapes=[pltpu.VMEM((Z_BLK, D_MODEL), BF16)],
        compiler_params=_params(2),
        name="inproj",
    )(h, w_in_t)


def _rope(x, c, s1, s2):
    return x * c + pltpu.roll(x, LANE - QK_ROPE // 4, 1) * s1 + pltpu.roll(x, QK_ROPE // 4, 1) * s2


def _qkv_kernel(z_ref, h_ref, wkr_ref, wuq_ref, wukv_ref, gq_ref, gkv_ref, *rest, rope):
    if rope:
        c_ref, s1_ref, s2_ref, q_ref, k_ref, v_ref = rest
    else:
        q_ref, k_ref, v_ref, ckvn_ref, kr_ref = rest[-5:]
    cq = z_ref[:, :Q_LORA]
    ckv = z_ref[:, Q_LORA:]
    q = jnp.dot(_rms(cq, gq_ref[...]).astype(BF16), wuq_ref[0], preferred_element_type=F32) * _EXP2_SCALE
    ckvn = _rms(ckv, gkv_ref[...])
    kv = jnp.dot(ckvn.astype(BF16), wukv_ref[0], preferred_element_type=F32)
    kr = lax.dot_general(h_ref[...], wkr_ref[0], _NT, preferred_element_type=F32)
    if rope:
        c, s1, s2 = c_ref[...], s1_ref[...], s2_ref[...]
        kr = _rope(kr, c, s1, s2)
    else:
        for b in range(ckvn_ref.shape[0]):
            rows = slice(b * SEQ, (b + 1) * SEQ)
            for s in range(ckvn_ref.shape[1]):
                ckvn_ref[b, s] = ckvn[rows, :] if s == 0 else jnp.zeros((SEQ, KV_LORA), F32)
                kr_ref[b, s] = kr[rows, :QK_ROPE] if s == 0 else jnp.zeros((SEQ, QK_ROPE), F32)
    kr_bf = kr.astype(BF16)
    for hd in range(MLA_HEADS):
        lo = hd * HEAD_SLOT
        q_ref[:, lo:lo + LANE] = q[:, lo:lo + LANE].astype(BF16)
        qr = q[:, lo + LANE:lo + HEAD_SLOT]
        if rope:
            qr = _rope(qr, c, s1, s2)
        q_ref[:, lo + LANE:lo + HEAD_SLOT] = qr.astype(BF16)
        k_ref[:, lo:lo + LANE] = kv[:, lo:lo + LANE].astype(BF16)
        k_ref[:, lo + LANE:lo + HEAD_SLOT] = kr_bf
        v_ref[:, hd * LANE:(hd + 1) * LANE] = kv[:, lo + LANE:lo + HEAD_SLOT].astype(BF16)


def _qkv(z, h, w_kr, w_uq_pad, w_ukv, g_q_l, g_kv_l, rope_tabs, l, group, stacks=None):
    tm = 512
    aliases = {}
    n_tok = N_P if group == 0 else N_S
    t0 = 0 if group == 0 else N_P // tm
    rope = group == 1
    in_specs = [pl.BlockSpec((tm, Z_BLK), lambda i: (i + t0, 0)),
                pl.BlockSpec((tm, D_MODEL), lambda i: (i + t0, 0)),
                pl.BlockSpec((1, LANE, D_MODEL), lambda i: (l, 0, 0)),
                pl.BlockSpec((1, Q_LORA, MLA_HEADS * HEAD_SLOT), lambda i: (l, 0, 0)),
                pl.BlockSpec((1, KV_LORA, MLA_HEADS * HEAD_SLOT), lambda i: (l, 0, 0)),
                pl.BlockSpec((1, Q_LORA), lambda i: (0, 0)),
                pl.BlockSpec((1, KV_LORA), lambda i: (0, 0))]
    args = [z, h, w_kr, w_uq_pad, w_ukv, g_q_l, g_kv_l]
    out_specs = [pl.BlockSpec((tm, MLA_HEADS * HEAD_SLOT), lambda i: (i, 0)),
                 pl.BlockSpec((tm, MLA_HEADS * HEAD_SLOT), lambda i: (i, 0)),
                 pl.BlockSpec((tm, MLA_W), lambda i: (i, 0))]
    out_shape = [jax.ShapeDtypeStruct((n_tok, MLA_HEADS * HEAD_SLOT), BF16),
                 jax.ShapeDtypeStruct((n_tok, MLA_HEADS * HEAD_SLOT), BF16),
                 jax.ShapeDtypeStruct((n_tok, MLA_W), BF16)]
    if rope:
        tiles_per_seq = DEC_SEQ // tm
        in_specs += [pl.BlockSpec((tm, LANE), lambda i: (i % tiles_per_seq, 0))] * 3
        args += list(rope_tabs)
    else:
        assert (l == 0) == (stacks is None)
        slices = DEPTH if stacks is None else 1
        out_specs += [pl.BlockSpec((tm // SEQ, slices, SEQ, KV_LORA), lambda i: (i, l, 0, 0)),
                      pl.BlockSpec((tm // SEQ, slices, SEQ, QK_ROPE), lambda i: (i, l, 0, 0))]
        out_shape += [jax.ShapeDtypeStruct((BATCH, DEPTH, SEQ, KV_LORA), F32),
                      jax.ShapeDtypeStruct((BATCH, DEPTH, SEQ, QK_ROPE), F32)]
        if stacks is not None:
            aliases = {len(args): 3, len(args) + 1: 4}
            in_specs += [pl.BlockSpec(memory_space=pl.ANY)] * 2
            args += list(stacks)
    return pl.pallas_call(
        functools.partial(_qkv_kernel, rope=rope),
        grid=(n_tok // tm,),
        in_specs=in_specs,
        out_specs=out_specs,
        out_shape=out_shape,
        input_output_aliases=aliases,
        compiler_params=_params(1),
        name="qkv_sample" if rope else "qkv_prompt",
    )(*args)


def _ctxkv_kernel(ckv_ref, kr_ref, w_ref, k_ref, v_ref):
    kv = jnp.dot(ckv_ref[0, 0].astype(BF16), w_ref[0], preferred_element_type=F32)
    kr = jnp.concatenate([kr_ref[0, 0], jnp.zeros((PAST_LEN, LANE - QK_ROPE), F32)], axis=1).astype(BF16)
    for hd in range(MLA_HEADS):
        lo = hd * HEAD_SLOT
        k_ref[0, 0, :, lo:lo + LANE] = kv[:, lo:lo + LANE].astype(BF16)
        k_ref[0, 0, :, lo + LANE:lo + HEAD_SLOT] = kr
        v_ref[0, 0, :, hd * LANE:(hd + 1) * LANE] = kv[:, lo + LANE:lo + HEAD_SLOT].astype(BF16)


def _ctxkv(cache_ckv, cache_krope, w_ukv):
    return pl.pallas_call(
        _ctxkv_kernel,
        grid=(DEPTH, DEC_BATCH),
        in_specs=[pl.BlockSpec((1, 1, PAST_LEN, KV_LORA), lambda l, b: (b, l, 0, 0)),
                  pl.BlockSpec((1, 1, PAST_LEN, QK_ROPE), lambda l, b: (b, l, 0, 0)),
                  pl.BlockSpec((1, KV_LORA, MLA_HEADS * HEAD_SLOT), lambda l, b: (l, 0, 0))],
        out_specs=[pl.BlockSpec((1, 1, PAST_LEN, MLA_HEADS * HEAD_SLOT), lambda l, b: (l, b, 0, 0)),
                   pl.BlockSpec((1, 1, PAST_LEN, MLA_W), lambda l, b: (l, b, 0, 0))],
        out_shape=[jax.ShapeDtypeStruct((DEPTH, DEC_BATCH, PAST_LEN, MLA_HEADS * HEAD_SLOT), BF16),
                   jax.ShapeDtypeStruct((DEPTH, DEC_BATCH, PAST_LEN, MLA_W), BF16)],
        compiler_params=_params(2),
        name="ctxkv",
    )(cache_ckv, cache_krope, w_ukv)


_EXP2_SCALE = (QK_NOPE + QK_ROPE) ** -0.5 * LOG2E


def _attn_prompt_kernel(q_ref, k_ref, v_ref, gp_ref, o_ref):
    for b in range(q_ref.shape[0] // SEQ):
        rows = slice(b * SEQ, (b + 1) * SEQ)
        for hd in range(MLA_HEADS):
            ks = slice(hd * HEAD_SLOT, (hd + 1) * HEAD_SLOT)
            vs = slice(hd * LANE, (hd + 1) * LANE)
            s = lax.dot_general(q_ref[rows, ks], k_ref[rows, ks], _NT, preferred_element_type=F32)
            e = jnp.exp2(s - jnp.max(s, axis=-1, keepdims=True))
            p = (e * (1.0 / jnp.sum(e, axis=-1, keepdims=True))).astype(BF16)
            o = jnp.dot(p, v_ref[rows, vs], preferred_element_type=F32)
            o_ref[rows, vs] = (gp_ref[rows, vs] * o).astype(BF16)


def _attn_prompt(q, k, v, z):
    tm = 4 * SEQ
    return pl.pallas_call(
        _attn_prompt_kernel,
        grid=(N_P // tm,),
        in_specs=[pl.BlockSpec((tm, MLA_HEADS * HEAD_SLOT), lambda i: (i, 0)),
                  pl.BlockSpec((tm, MLA_HEADS * HEAD_SLOT), lambda i: (i, 0)),
                  pl.BlockSpec((tm, MLA_W), lambda i: (i, 0)),
                  pl.BlockSpec((tm, Z_BLK), lambda i: (i, ZB_GP_MLA))],
        out_specs=pl.BlockSpec((tm, MLA_W), lambda i: (i, 0)),
        out_shape=jax.ShapeDtypeStruct((N_P, MLA_W), BF16),
        compiler_params=_params(1),
        name="attn_prompt",
    )(q, k, v, z)


def _attn_sample_kernel(q_ref, k_ref, v_ref, kc_ref, vc_ref, gp_ref, o_ref, *, kchunk, heads):
    def lane_fold(x):
        out = x[:, :LANE]
        for c0 in range(LANE, x.shape[1], LANE):
            out = out + x[:, c0:c0 + LANE]
        return out

    for hd in range(heads):
        ks = slice(hd * HEAD_SLOT, (hd + 1) * HEAD_SLOT)
        vs = slice(hd * LANE, (hd + 1) * LANE)
        q = q_ref[:, ks]
        chunks = [(kc_ref[0, 0, :, ks], vc_ref[0, 0, :, vs])]
        chunks += [(k_ref[r:r + kchunk, ks], v_ref[r:r + kchunk, vs]) for r in range(0, DEC_SEQ, kchunk)]
        m = l = acc = None
        for k, v in chunks:
            s = lax.dot_general(q, k, _NT, preferred_element_type=F32)
            m_chunk = jnp.max(s, axis=-1, keepdims=True)
            if m is None:
                m = m_chunk
                e = jnp.exp2(s - m)
                l = lane_fold(e)
                acc = jnp.dot(e.astype(BF16), v, preferred_element_type=F32)
            else:
                m_new = jnp.maximum(m, m_chunk)
                alpha = jnp.exp2(m - m_new)
                e = jnp.exp2(s - m_new)
                l = alpha * l + lane_fold(e)
                acc = alpha * acc + jnp.dot(e.astype(BF16), v, preferred_element_type=F32)
                m = m_new
        inv = 1.0 / jnp.sum(l, axis=-1, keepdims=True)
        o_ref[:, vs] = (gp_ref[:, vs] * (acc * inv)).astype(BF16)


def _attn_sample(q, k, v, kc, vc, z, l):
    tq = 512
    kchunk = PAST_LEN
    heads = 4
    tiles_per_seq = DEC_SEQ // tq
    z_tile0 = N_P // tq
    gp_blk0 = ZB_GP_MLA * Z_BLK // (heads * LANE)
    return pl.pallas_call(
        functools.partial(_attn_sample_kernel, kchunk=kchunk, heads=heads),
        grid=(DEC_BATCH, MLA_HEADS // heads, tiles_per_seq),
        in_specs=[pl.BlockSpec((tq, heads * HEAD_SLOT), lambda b, hb, t: (b * tiles_per_seq + t, hb)),
                  pl.BlockSpec((DEC_SEQ, heads * HEAD_SLOT), lambda b, hb, t: (b, hb)),
                  pl.BlockSpec((DEC_SEQ, heads * LANE), lambda b, hb, t: (b, hb)),
                  pl.BlockSpec((1, 1, PAST_LEN, heads * HEAD_SLOT), lambda b, hb, t: (l, b, 0, hb)),
                  pl.BlockSpec((1, 1, PAST_LEN, heads * LANE), lambda b, hb, t: (l, b, 0, hb)),
                  pl.BlockSpec((tq, heads * LANE),
                               lambda b, hb, t: (z_tile0 + b * tiles_per_seq + t, gp_blk0 + hb))],
        out_specs=pl.BlockSpec((tq, heads * LANE), lambda b, hb, t: (b * tiles_per_seq + t, hb)),
        out_shape=jax.ShapeDtypeStruct((N_S, MLA_W), BF16),
        compiler_params=_params(3),
        name="attn_sample",
    )(q, k, v, kc, vc, z)


def _sgu_kernel(u_ref, v_ref, gp_ref, g_ref, w_ref, b_ref, o_ref, *, chunks):
    for c in range(chunks):
        rows = slice(c * SGU_CHUNK, (c + 1) * SGU_CHUNK)
        vn = _rms(v_ref[rows, :], g_ref[...]).astype(BF16)
        for g in range(SGU_GROUPS):
            cols = slice(g * LANE, (g + 1) * LANE)
            s = jnp.dot(w_ref[0, g], vn[:, cols], preferred_element_type=F32) + b_ref[0, :, cols]
            o_ref[rows, cols] = (gp_ref[rows, cols] * (u_ref[rows, cols] * s)).astype(BF16)


def _sgu(z, g_sgu_l, w_sgu, b_sgu_exp, l):
    tm = 1024
    return pl.pallas_call(
        functools.partial(_sgu_kernel, chunks=tm // SGU_CHUNK),
        grid=(N_TOK // tm,),
        in_specs=[pl.BlockSpec((tm, Z_BLK), lambda i: (i, ZB_U)),
                  pl.BlockSpec((tm, Z_BLK), lambda i: (i, ZB_VS)),
                  pl.BlockSpec((tm, Z_BLK), lambda i: (i, ZB_GP_SGU)),
                  pl.BlockSpec((1, SGU_W), lambda i: (0, 0)),
                  pl.BlockSpec((1, SGU_GROUPS, SGU_CHUNK, SGU_CHUNK), lambda i: (l, 0, 0, 0)),
                  pl.BlockSpec((1, SGU_CHUNK, SGU_W), lambda i: (l, 0, 0))],
        out_specs=pl.BlockSpec((tm, SGU_W), lambda i: (i, 0)),
        out_shape=jax.ShapeDtypeStruct((N_TOK, SGU_W), BF16),
        compiler_params=_params(1),
        name="sgu",
    )(z, z, z, g_sgu_l, w_sgu, b_sgu_exp)


def _log_sigmoid(x):
    return -(jnp.maximum(-x, 0.0) + jnp.log1p(jnp.exp(-jnp.abs(x))))


def _ret_kernel(q_ref, k_ref, v_ref, gp_ref, dec_ref, g_ref, *rest, block, n_chunks, heads, has_state,
                emit_state):
    rest = list(rest)
    r0_ref = rest.pop(0) if has_state else None
    rout_ref = rest.pop() if emit_state else None
    o_ref = rest.pop()
    C = block
    shape = (RET_DK, RET_DV)
    ii = lax.broadcasted_iota(jnp.int32, (C, C), 0).astype(F32)
    jj = lax.broadcasted_iota(jnp.int32, (C, C), 1).astype(F32)
    ri = lax.broadcasted_iota(jnp.int32, (C, RET_DV), 0).astype(F32)
    cj = lax.broadcasted_iota(jnp.int32, (RET_DK, C), 1).astype(F32)
    lower = ii >= jj
    k_scale = RET_DK ** -0.5

    for hd in range(heads):
        cols = slice(hd * LANE, (hd + 1) * LANE)
        lg_f8 = _log_sigmoid(dec_ref[0, 0, hd])
        lg_b8 = _log_sigmoid(dec_ref[0, 1, hd])

        def spread(x8, rows, width):
            return jnp.tile(x8, (rows // 8, width // LANE))

        dec = jnp.where(lower, jnp.exp(jnp.where(lower, ii - jj, 0.0) * spread(lg_f8, C, C)),
                        jnp.exp(jnp.where(lower, 0.0, jj - ii) * spread(lg_b8, C, C)))
        xi_f = jnp.exp((ri + 1.0) * spread(lg_f8, C, RET_DV))
        xi_b = jnp.exp(((C - 1.0 - ri) + 1.0) * spread(lg_b8, C, RET_DV))
        zeta_f = jnp.exp((C - 1.0 - cj) * spread(lg_f8, RET_DK, C))
        zeta_b = jnp.exp((C - 1.0 - (C - 1.0 - cj)) * spread(lg_b8, RET_DK, C))
        gc_f = jnp.exp(C * spread(lg_f8, RET_DK, RET_DV))
        gc_b = jnp.exp(C * spread(lg_b8, RET_DK, RET_DV))

        def rows_of(c):
            return slice(c * C, (c + 1) * C)

        kvs = []
        for c in range(n_chunks):
            k_t = jnp.transpose(k_ref[rows_of(c), cols] * k_scale)
            kz = jnp.concatenate([k_t * zeta_f, k_t * zeta_b], axis=0).astype(BF16)
            kvs.append(jnp.dot(kz, v_ref[rows_of(c), cols].astype(BF16), preferred_element_type=F32))

        rfs, rbs = [None] * n_chunks, [None] * n_chunks
        r_f = r0_ref[0, 0, 0, hd] if has_state else jnp.zeros(shape, F32)
        for c in range(n_chunks):
            rfs[c] = r_f.astype(BF16)
            r_f = gc_f * r_f + kvs[c][:RET_DK, :]
        r_b = r0_ref[0, 0, 1, hd] if has_state else jnp.zeros(shape, F32)
        for c in reversed(range(n_chunks)):
            rbs[c] = r_b.astype(BF16)
            r_b = gc_b * r_b + kvs[c][RET_DK:, :]
        if emit_state:
            for s in range(rout_ref.shape[1]):
                rout_ref[0, s, 0, hd] = r_f if s == 0 else jnp.zeros(shape, F32)
                rout_ref[0, s, 1, hd] = r_b if s == 0 else jnp.zeros(shape, F32)

        for c in range(n_chunks):
            q = q_ref[rows_of(c), cols].astype(BF16)
            k = (k_ref[rows_of(c), cols] * k_scale).astype(BF16)
            att = lax.dot_general(q, k, _NT, preferred_element_type=F32) * dec
            o = jnp.dot(att.astype(BF16), v_ref[rows_of(c), cols].astype(BF16), preferred_element_type=F32)
            use_f = has_state or c > 0
            use_b = has_state or c < n_chunks - 1
            if use_f and use_b:
                inter = jnp.dot(q, jnp.concatenate([rfs[c], rbs[c]], axis=1), preferred_element_type=F32)
                o = o + inter[:, :RET_DV] * xi_f + inter[:, RET_DV:] * xi_b
            elif use_f:
                o = o + jnp.dot(q, rfs[c], preferred_element_type=F32) * xi_f
            elif use_b:
                o = o + jnp.dot(q, rbs[c], preferred_element_type=F32) * xi_b
            mu = jnp.mean(o, axis=-1, keepdims=True)
            var = jnp.mean(jnp.square(o - mu), axis=-1, keepdims=True)
            y = (o - mu) * lax.rsqrt(var + EPS) * g_ref[:, cols]
            o_ref[rows_of(c), cols] = (gp_ref[rows_of(c), cols] * y).astype(BF16)


def _retention(z, decay_b, g_ret_l, state_ret, l, group, stack=None):
    seq = SEQ if group == 0 else DEC_SEQ
    nb = BATCH if group == 0 else DEC_BATCH
    heads = RET_HEADS if group == 0 else 2
    hblk = RET_HEADS // heads
    b0 = 0 if group == 0 else N_P // seq
    has_state = group == 1
    emit_state = group == 0
    block = 2 * RET_CHUNK
    n_chunks = seq // block
    width = heads * LANE

    def zspec(zb):
        return pl.BlockSpec((seq, width), lambda b, hb: (b0 + b, zb * Z_BLK // width + hb))

    in_specs = [zspec(ZB_RQ), zspec(ZB_RK), zspec(ZB_RV), zspec(ZB_GP_RET),
                pl.BlockSpec((1, 2, heads, 8, LANE), lambda b, hb: (l, 0, hb, 0, 0)),
                pl.BlockSpec((1, width), lambda b, hb: (0, hb))]
    args = [z, z, z, z, decay_b, g_ret_l]
    state_block = (1, 1, 2, heads, RET_DK, RET_DV)
    if has_state:
        in_specs.append(pl.BlockSpec(state_block, lambda b, hb: (b, l, 0, hb, 0, 0)))
        args.append(state_ret)
    out_specs = [pl.BlockSpec((seq, width), lambda b, hb: (b, hb))]
    out_shape = [jax.ShapeDtypeStruct((nb * seq, RET_W), BF16)]
    aliases = {}
    if emit_state:
        assert (l == 0) == (stack is None)
        slices = DEPTH if stack is None else 1
        out_specs.append(pl.BlockSpec((1, slices) + state_block[2:], lambda b, hb: (b, l, 0, hb, 0, 0)))
        out_shape.append(jax.ShapeDtypeStruct((nb, DEPTH, 2, RET_HEADS, RET_DK, RET_DV), F32))
        if stack is not None:
            aliases = {len(args): 1}
            in_specs.append(pl.BlockSpec(memory_space=pl.ANY))
            args.append(stack)
    return pl.pallas_call(
        functools.partial(_ret_kernel, block=block, n_chunks=n_chunks, heads=heads, has_state=has_state,
                          emit_state=emit_state),
        grid=(nb, hblk),
        in_specs=in_specs,
        out_specs=out_specs,
        out_shape=out_shape,
        input_output_aliases=aliases,
        compiler_params=_params(2),
        name="ret_sample" if has_state else "ret_prompt",
    )(*args)


def _tail_kernel(ap_m, as_m, a_sgu, ap_r, as_r, m_ref, wm_ref, ws_ref, wr_ref, wo_ref,
                 gpost_ref, mod_ref, *rest, tiles_p, first, last):
    i = pl.program_id(0)
    is_p = i < tiles_p
    rest = list(rest)
    if first:
        x = jnp.where(is_p, rest.pop(0)[...], rest.pop(0)[...])
    else:
        x = rest.pop(0)[...]
    a_mla = jnp.where(is_p, ap_m[...], as_m[...])
    a_ret = jnp.where(is_p, ap_r[...], as_r[...])
    y_mla = jnp.dot(a_mla, wm_ref[0], preferred_element_type=F32)
    y = m_ref[:, :D_MODEL] * y_mla
    y_sgu = jnp.dot(a_sgu[...], ws_ref[0], preferred_element_type=F32)
    y = y + m_ref[:, D_MODEL:2 * D_MODEL] * y_sgu
    y_ret = jnp.dot(a_ret, wr_ref[0], preferred_element_type=F32)
    y = y + m_ref[:, 2 * D_MODEL:] * y_ret
    out = jnp.dot(y.astype(BF16), wo_ref[0], preferred_element_type=F32)
    x_new = x + mod_ref[0, 2:3, :] * _rms(out, gpost_ref[...])
    if last:
        yp_ref, ys_ref = rest

        @pl.when(is_p)
        def _():
            yp_ref[...] = x_new

        @pl.when(jnp.logical_not(is_p))
        def _():
            ys_ref[...] = x_new
    else:
        gpre_ref, modn_ref, x_out, h_out = rest
        x_out[...] = x_new
        h_out[...] = _prenorm_value(x_new, gpre_ref[...], modn_ref)


def _tail(a_mla_p, a_mla_s, a_sgu, a_ret_p, a_ret_s, z, w_br_mla, w_br_sgu, w_br_ret, w_out, xs, g_post_l,
          g_pre_next, mod3, l):
    tm = 256
    tiles_p = N_P // tm
    last = g_pre_next is None
    first = len(xs) == 2

    def p_spec():
        return pl.BlockSpec((tm, 1024), lambda i: (jnp.minimum(i, tiles_p - 1), 0))

    def s_spec():
        return pl.BlockSpec((tm, 1024), lambda i: (jnp.maximum(i - tiles_p, 0), 0))

    def w_spec(rows):
        return pl.BlockSpec((1, rows, D_MODEL), lambda i: (l, 0, 0), pipeline_mode=pl.Buffered(1))

    def row_spec():
        return pl.BlockSpec((tm, D_MODEL), lambda i: (i, 0))

    in_specs = [p_spec(), s_spec(), pl.BlockSpec((tm, 1024), lambda i: (i, 0)), p_spec(), s_spec()]
    in_specs += [pl.BlockSpec((pl.Element(tm), pl.Element(N_BRANCH * D_MODEL)),
                              lambda i: (pl.multiple_of(i * tm, tm), ZB_MERGE * Z_BLK))]
    in_specs += [w_spec(1024), w_spec(1024), w_spec(1024), w_spec(D_MODEL),
                 pl.BlockSpec((1, D_MODEL), lambda i: (0, 0)), _mod_spec(l, tm)]
    args = [a_mla_p, a_mla_s, a_sgu, a_ret_p, a_ret_s, z]
    args += [w_br_mla, w_br_sgu, w_br_ret, w_out, g_post_l, mod3]
    if first:
        in_specs += [pl.BlockSpec((tm, D_MODEL), lambda i: (jnp.minimum(i, tiles_p - 1), 0)),
                     pl.BlockSpec((tm, D_MODEL), lambda i: (jnp.maximum(i - tiles_p, 0), 0))]
    else:
        in_specs += [row_spec()]
    args += list(xs)
    if last:
        out_specs = [pl.BlockSpec((tm, D_MODEL), lambda i: (jnp.minimum(i, tiles_p - 1), 0)),
                     pl.BlockSpec((tm, D_MODEL), lambda i: (jnp.maximum(i - tiles_p, 0), 0))]
        out_shape = [jax.ShapeDtypeStruct((N_P, D_MODEL), F32), jax.ShapeDtypeStruct((N_S, D_MODEL), F32)]
    else:
        in_specs += [pl.BlockSpec((1, D_MODEL), lambda i: (0, 0)), _mod_spec(l + 1, tm)]
        args += [g_pre_next, mod3]
        out_specs = [row_spec(), row_spec()]
        out_shape = [jax.ShapeDtypeStruct((N_TOK, D_MODEL), F32), jax.ShapeDtypeStruct((N_TOK, D_MODEL), BF16)]
    return pl.pallas_call(
        functools.partial(_tail_kernel, tiles_p=tiles_p, first=first, last=last),
        grid=(N_TOK // tm,),
        in_specs=in_specs,
        out_specs=out_specs,
        out_shape=out_shape,
        compiler_params=_params(1),
        name="tail_last" if last else "tail",
    )(*args)


def _rope_tables():
    f32 = np.float32
    pos = np.arange(DEC_SEQ)
    row = (pos // GRID_W).astype(f32)
    col = (pos % GRID_W).astype(f32)
    n_freq = QK_ROPE // 4
    inv = f32(ROPE_BASE) ** (-np.arange(n_freq, dtype=f32) / f32(n_freq))
    ang_r = row[:, None] * inv
    ang_c = col[:, None] * inv
    zeros = np.zeros((DEC_SEQ, n_freq), f32)
    pad = np.zeros((DEC_SEQ, LANE - QK_ROPE), f32)
    cr, sr, cc, sc = np.cos(ang_r), np.sin(ang_r), np.cos(ang_c), np.sin(ang_c)
    c = np.concatenate([cr, cr, cc, cc, pad], axis=1)
    s1 = np.concatenate([-sr, zeros, -sc, zeros, pad], axis=1)
    s2 = np.concatenate([zeros, sr, zeros, sc, pad], axis=1)
    return tuple(jnp.asarray(t.astype(f32)) for t in (c, s1, s2))


def kernel(x_prompt, x_sample, cache_ckv, cache_krope, state_ret, c, c_ctx, w_mod, b_mod, g_pre, g_post, w_in,
           g_q, g_kv, w_uq, w_ukv, g_sgu, w_sgu, b_sgu, ret_decay, g_ret, w_br_mla, w_br_sgu, w_br_ret, w_out):
    cond8 = jnp.concatenate([c_ctx[None, :], c, jnp.zeros((COND_ROWS - 1 - DEC_BATCH, D_MODEL), F32)], axis=0)
    mod3 = _modulation(cond8, w_mod, b_mod).reshape(DEPTH * COND_ROWS, 3, D_MODEL)
    w_in_t = jnp.swapaxes(w_in, 1, 2)
    w_kr = jnp.pad(w_in_t[:, KR_COL:KR_COL + QK_ROPE, :], ((0, 0), (0, LANE - QK_ROPE), (0, 0))).astype(BF16)
    w_uq_pad = jnp.pad(w_uq.reshape(DEPTH, Q_LORA, MLA_HEADS, QK_NOPE + QK_ROPE),
                       ((0, 0), (0, 0), (0, 0), (0, HEAD_SLOT - QK_NOPE - QK_ROPE))
                       ).reshape(DEPTH, Q_LORA, MLA_HEADS * HEAD_SLOT).astype(BF16)
    w_ukv_bf = w_ukv.astype(BF16)
    w_sgu_bf = w_sgu.astype(BF16)
    w_br = [w.astype(BF16) for w in (w_br_mla, w_br_sgu, w_br_ret)]
    w_out_bf = w_out.astype(BF16)
    b_sgu_exp = jnp.repeat(b_sgu, LANE, axis=2)
    decay_b = jnp.broadcast_to(ret_decay[:, :, :, None, None], (DEPTH, 2, RET_HEADS, 8, LANE))
    rope_tabs = _rope_tables()
    kc, vc = _ctxkv(cache_ckv, cache_krope, w_ukv_bf)

    xs = (x_prompt.reshape(N_P, D_MODEL), x_sample.reshape(N_S, D_MODEL))
    h = _prenorm(*xs, g_pre[0][None, :], mod3)
    cache_stacks = ret_stack = None
    for l in range(DEPTH):
        z = _inproj(h, w_in_t, l)
        q_p, k_p, v_p, *cache_stacks = _qkv(z, h, w_kr, w_uq_pad, w_ukv_bf, g_q[l][None, :], g_kv[l][None, :], None,
                                            l, 0, cache_stacks)
        a_mla_p = _attn_prompt(q_p, k_p, v_p, z)
        q_s, k_s, v_s = _qkv(z, h, w_kr, w_uq_pad, w_ukv_bf, g_q[l][None, :], g_kv[l][None, :], rope_tabs, l, 1)
        a_mla_s = _attn_sample(q_s, k_s, v_s, kc, vc, z, l)
        a_sgu = _sgu(z, g_sgu[l][None, :], w_sgu_bf, b_sgu_exp, l)
        a_ret_p, ret_stack = _retention(z, decay_b, g_ret[l][None, :], None, l, 0, ret_stack)
        (a_ret_s,) = _retention(z, decay_b, g_ret[l][None, :], state_ret, l, 1)
        g_pre_next = g_pre[l + 1][None, :] if l + 1 < DEPTH else None
        outs = _tail(a_mla_p, a_mla_s, a_sgu, a_ret_p, a_ret_s, z, *w_br, w_out_bf, xs, g_post[l][None, :],
                     g_pre_next, mod3, l)
        if l + 1 < DEPTH:
            x, h = outs
            xs = (x,)
        else:
            y_p, y_s = outs
    return (y_p.reshape(BATCH, SEQ, D_MODEL), y_s.reshape(DEC_BATCH, DEC_SEQ, D_MODEL), *cache_stacks, ret_stack)
```

```python
import functools

import jax
import jax.numpy as jnp
import numpy as np
from jax import lax
from jax.experimental import pallas as pl
from jax.experimental.pallas import tpu as pltpu

F32 = jnp.float32
BF16 = jnp.bfloat16

D_MODEL = 2048
BATCH = 16
SEQ = 256
DEPTH = 4
DEC_BATCH = 2
DEC_SEQ = 2048
PAST_LEN = 256
GRID_W = 64
EPS = 1e-6
ROPE_BASE = 10000.0

MLA_HEADS = 8
QK_NOPE = 128
QK_ROPE = 64
V_HEAD = 128
Q_LORA = 512
KV_LORA = 512
MLA_W = MLA_HEADS * V_HEAD
SGU_CHUNK = 128
SGU_GROUPS = 8
SGU_W = 1024
RET_HEADS = 8
RET_DK = 128
RET_DV = 128
RET_W = RET_HEADS * RET_DV
RET_CHUNK = 128
N_BRANCH = 3

LANE = 128
N_P = BATCH * SEQ
N_S = DEC_BATCH * DEC_SEQ
N_TOK = N_P + N_S
COND_ROWS = 8
HEAD_SLOT = 2 * LANE
KR_COL = Q_LORA + KV_LORA
Z_BLK = 1024
N_ZBLK = 15
ZB_GP_MLA, ZB_U, ZB_VS, ZB_GP_SGU, ZB_RQ, ZB_RK, ZB_RV, ZB_GP_RET, ZB_MERGE = 1, 2, 3, 4, 5, 6, 7, 8, 9
VMEM_LIMIT = 56 * 1024 * 1024
LOG2E = 1.4426950408889634

_NT = (((1,), (1,)), ((), ()))


def _params(n_axes, vmem=VMEM_LIMIT):
    return pltpu.CompilerParams(dimension_semantics=("arbitrary",) * n_axes, vmem_limit_bytes=vmem)


def _rms(x, g):
    return x * lax.rsqrt(jnp.mean(x * x, axis=-1, keepdims=True) + EPS) * g


def _silu(x):
    return x * jax.nn.sigmoid(x)


def _cond_row(tile, tm):
    return jnp.where(tile < N_P // tm, 0, 1 + (tile - N_P // tm) // (DEC_SEQ // tm))


def _mod_spec(l, tm):
    return pl.BlockSpec((1, 3, D_MODEL), lambda i: (l * COND_ROWS + _cond_row(i, tm), 0, 0))


def _prenorm_value(x, g, mod_ref):
    return (_rms(x, g) * (1.0 + mod_ref[0, 1:2, :]) + mod_ref[0, 0:1, :]).astype(BF16)


def _mod_kernel(cond_ref, w_ref, b_ref, o_ref):
    s = _silu(cond_ref[...]).astype(BF16)
    o_ref[0] = jnp.dot(s, w_ref[0].astype(BF16), preferred_element_type=F32) + b_ref[0]


def _modulation(cond8, w_mod, b_mod):
    tn = 2048
    return pl.pallas_call(
        _mod_kernel,
        grid=(DEPTH, 3 * D_MODEL // tn),
        in_specs=[pl.BlockSpec((COND_ROWS, D_MODEL), lambda l, n: (0, 0)),
                  pl.BlockSpec((1, D_MODEL, tn), lambda l, n: (l, 0, n)),
                  pl.BlockSpec((1, 1, tn), lambda l, n: (l, 0, n))],
        out_specs=pl.BlockSpec((1, COND_ROWS, tn), lambda l, n: (l, 0, n)),
        out_shape=jax.ShapeDtypeStruct((DEPTH, COND_ROWS, 3 * D_MODEL), F32),
        compiler_params=_params(2),
        name="modulation",
    )(cond8, w_mod, b_mod.reshape(DEPTH, 1, 3 * D_MODEL))


def _prenorm_kernel(xp_ref, xs_ref, g_ref, mod_ref, h_ref, *, tiles_p):
    x = jnp.where(pl.program_id(0) < tiles_p, xp_ref[...], xs_ref[...])
    h_ref[...] = _prenorm_value(x, g_ref[...], mod_ref)


def _prenorm(x_p, x_s, g_pre_l, mod3):
    tm = 512
    tiles_p = N_P // tm
    return pl.pallas_call(
        functools.partial(_prenorm_kernel, tiles_p=tiles_p),
        grid=(N_TOK // tm,),
        in_specs=[pl.BlockSpec((tm, D_MODEL), lambda i: (jnp.minimum(i, tiles_p - 1), 0)),
                  pl.BlockSpec((tm, D_MODEL), lambda i: (jnp.maximum(i - tiles_p, 0), 0)),
                  pl.BlockSpec((1, D_MODEL), lambda i: (0, 0)),
                  _mod_spec(0, tm)],
        out_specs=pl.BlockSpec((tm, D_MODEL), lambda i: (i, 0)),
        out_shape=jax.ShapeDtypeStruct((N_TOK, D_MODEL), BF16),
        compiler_params=_params(1),
        name="prenorm",
    )(x_p, x_s, g_pre_l, mod3)


def _inproj_kernel(h_ref, w_ref, o_ref, wbf_ref):
    j = pl.program_id(0)
    first_tile = pl.program_id(1) == 0
    is_silu = jnp.logical_or(jnp.logical_or(j == ZB_GP_MLA, j == ZB_GP_SGU), j == ZB_GP_RET)
    is_act = jnp.logical_or(is_silu, j >= ZB_MERGE)

    def weights(cast):
        if not cast:
            return wbf_ref[...]
        w = w_ref[0].astype(BF16)
        wbf_ref[...] = w
        return w

    for cast in (True, False):
        on_tile = first_tile if cast else jnp.logical_not(first_tile)

        @pl.when(jnp.logical_and(on_tile, is_act))
        def _():
            acc = lax.dot_general(h_ref[...], weights(cast), _NT, preferred_element_type=F32)
            sig = 0.5 * jnp.tanh(0.5 * acc) + 0.5
            o_ref[...] = jnp.where(is_silu, acc * sig, sig)

        @pl.when(jnp.logical_and(on_tile, jnp.logical_not(is_act)))
        def _():
            o_ref[...] = lax.dot_general(h_ref[...], weights(cast), _NT, preferred_element_type=F32)


def _inproj(h, w_in_t, l):
    tm = 1024
    return pl.pallas_call(
        _inproj_kernel,
        grid=(N_ZBLK, N_TOK // tm),
        in_specs=[pl.BlockSpec((tm, D_MODEL), lambda j, i: (i, 0)),
                  pl.BlockSpec((pl.Element(1), pl.Element(Z_BLK), pl.Element(D_MODEL)),
                               lambda j, i: (l, pl.multiple_of(Z_BLK * j + jnp.where(j > 0, QK_ROPE, 0), QK_ROPE),
                                             0))],
        out_specs=pl.BlockSpec((tm, Z_BLK), lambda j, i: (i, j)),
        out_shape=jax.ShapeDtypeStruct((N_TOK, N_ZBLK * Z_BLK), F32),
        scratch_shapes=[pltpu.VMEM((Z_BLK, D_MODEL), BF16)],
        compiler_params=_params(2),
        name="inproj",
    )(h, w_in_t)


def _rope(x, c, s1, s2):
    return x * c + pltpu.roll(x, LANE - QK_ROPE // 4, 1) * s1 + pltpu.roll(x, QK_ROPE // 4, 1) * s2


def _qkv_kernel(z_ref, h_ref, wkr_ref, wuq_ref, wukv_ref, gq_ref, gkv_ref, *rest, rope):
    if rope:
        c_ref, s1_ref, s2_ref, q_ref, k_ref, v_ref = rest
    else:
        q_ref, k_ref, v_ref, ckvn_ref, kr_ref = rest[-5:]
    cq = z_ref[:, :Q_LORA]
    ckv = z_ref[:, Q_LORA:]
    q = jnp.dot(_rms(cq, gq_ref[...]).astype(BF16), wuq_ref[0], preferred_element_type=F32) * _EXP2_SCALE
    ckvn = _rms(ckv, gkv_ref[...])
    kv = jnp.dot(ckvn.astype(BF16), wukv_ref[0], preferred_element_type=F32)
    kr = lax.dot_general(h_ref[...], wkr_ref[0], _NT, preferred_element_type=F32)
    if rope:
        c, s1, s2 = c_ref[...], s1_ref[...], s2_ref[...]
        kr = _rope(kr, c, s1, s2)
    else:
        for b in range(ckvn_ref.shape[0]):
            rows = slice(b * SEQ, (b + 1) * SEQ)
            for s in range(ckvn_ref.shape[1]):
                ckvn_ref[b, s] = ckvn[rows, :] if s == 0 else jnp.zeros((SEQ, KV_LORA), F32)
                kr_ref[b, s] = kr[rows, :QK_ROPE] if s == 0 else jnp.zeros((SEQ, QK_ROPE), F32)
    kr_bf = kr.astype(BF16)
    for hd in range(MLA_HEADS):
        lo = hd * HEAD_SLOT
        q_ref[:, lo:lo + LANE] = q[:, lo:lo + LANE].astype(BF16)
        qr = q[:, lo + LANE:lo + HEAD_SLOT]
        if rope:
            qr = _rope(qr, c, s1, s2)
        q_ref[:, lo + LANE:lo + HEAD_SLOT] = qr.astype(BF16)
        k_ref[:, lo:lo + LANE] = kv[:, lo:lo + LANE].astype(BF16)
        k_ref[:, lo + LANE:lo + HEAD_SLOT] = kr_bf
        v_ref[:, hd * LANE:(hd + 1) * LANE] = kv[:, lo + LANE:lo + HEAD_SLOT].astype(BF16)


def _qkv(z, h, w_kr, w_uq_pad, w_ukv, g_q_l, g_kv_l, rope_tabs, l, group, stacks=None):
    tm = 512
    aliases = {}
    n_tok = N_P if group == 0 else N_S
    t0 = 0 if group == 0 else N_P // tm
    rope = group == 1
    in_specs = [pl.BlockSpec((tm, Z_BLK), lambda i: (i + t0, 0)),
                pl.BlockSpec((tm, D_MODEL), lambda i: (i + t0, 0)),
                pl.BlockSpec((1, LANE, D_MODEL), lambda i: (l, 0, 0)),
                pl.BlockSpec((1, Q_LORA, MLA_HEADS * HEAD_SLOT), lambda i: (l, 0, 0)),
                pl.BlockSpec((1, KV_LORA, MLA_HEADS * HEAD_SLOT), lambda i: (l, 0, 0)),
                pl.BlockSpec((1, Q_LORA), lambda i: (0, 0)),
                pl.BlockSpec((1, KV_LORA), lambda i: (0, 0))]
    args = [z, h, w_kr, w_uq_pad, w_ukv, g_q_l, g_kv_l]
    out_specs = [pl.BlockSpec((tm, MLA_HEADS * HEAD_SLOT), lambda i: (i, 0)),
                 pl.BlockSpec((tm, MLA_HEADS * HEAD_SLOT), lambda i: (i, 0)),
                 pl.BlockSpec((tm, MLA_W), lambda i: (i, 0))]
    out_shape = [jax.ShapeDtypeStruct((n_tok, MLA_HEADS * HEAD_SLOT), BF16),
                 jax.ShapeDtypeStruct((n_tok, MLA_HEADS * HEAD_SLOT), BF16),
                 jax.ShapeDtypeStruct((n_tok, MLA_W), BF16)]
    if rope:
        tiles_per_seq = DEC_SEQ // tm
        in_specs += [pl.BlockSpec((tm, LANE), lambda i: (i % tiles_per_seq, 0))] * 3
        args += list(rope_tabs)
    else:
        assert (l == 0) == (stacks is None)
        slices = DEPTH if stacks is None else 1
        out_specs += [pl.BlockSpec((tm // SEQ, slices, SEQ, KV_LORA), lambda i: (i, l, 0, 0)),
                      pl.BlockSpec((tm // SEQ, slices, SEQ, QK_ROPE), lambda i: (i, l, 0, 0))]
        out_shape += [jax.ShapeDtypeStruct((BATCH, DEPTH, SEQ, KV_LORA), F32),
                      jax.ShapeDtypeStruct((BATCH, DEPTH, SEQ, QK_ROPE), F32)]
        if stacks is not None:
            aliases = {len(args): 3, len(args) + 1: 4}
            in_specs += [pl.BlockSpec(memory_space=pl.ANY)] * 2
            args += list(stacks)
    return pl.pallas_call(
        functools.partial(_qkv_kernel, rope=rope),
        grid=(n_tok // tm,),
        in_specs=in_specs,
        out_specs=out_specs,
        out_shape=out_shape,
        input_output_aliases=aliases,
        compiler_params=_params(1),
        name="qkv_sample" if rope else "qkv_prompt",
    )(*args)


def _ctxkv_kernel(ckv_ref, kr_ref, w_ref, k_ref, v_ref):
    kv = jnp.dot(ckv_ref[0, 0].astype(BF16), w_ref[0], preferred_element_type=F32)
    kr = jnp.concatenate([kr_ref[0, 0], jnp.zeros((PAST_LEN, LANE - QK_ROPE), F32)], axis=1).astype(BF16)
    for hd in range(MLA_HEADS):
        lo = hd * HEAD_SLOT
        k_ref[0, 0, :, lo:lo + LANE] = kv[:, lo:lo + LANE].astype(BF16)
        k_ref[0, 0, :, lo + LANE:lo + HEAD_SLOT] = kr
        v_ref[0, 0, :, hd * LANE:(hd + 1) * LANE] = kv[:, lo + LANE:lo + HEAD_SLOT].astype(BF16)


def _ctxkv(cache_ckv, cache_krope, w_ukv):
    return pl.pallas_call(
        _ctxkv_kernel,
        grid=(DEPTH, DEC_BATCH),
        in_specs=[pl.BlockSpec((1, 1, PAST_LEN, KV_LORA), lambda l, b: (b, l, 0, 0)),
                  pl.BlockSpec((1, 1, PAST_LEN, QK_ROPE), lambda l, b: (b, l, 0, 0)),
                  pl.BlockSpec((1, KV_LORA, MLA_HEADS * HEAD_SLOT), lambda l, b: (l, 0, 0))],
        out_specs=[pl.BlockSpec((1, 1, PAST_LEN, MLA_HEADS * HEAD_SLOT), lambda l, b: (l, b, 0, 0)),
                   pl.BlockSpec((1, 1, PAST_LEN, MLA_W), lambda l, b: (l, b, 0, 0))],
        out_shape=[jax.ShapeDtypeStruct((DEPTH, DEC_BATCH, PAST_LEN, MLA_HEADS * HEAD_SLOT), BF16),
                   jax.ShapeDtypeStruct((DEPTH, DEC_BATCH, PAST_LEN, MLA_W), BF16)],
        compiler_params=_params(2),
        name="ctxkv",
    )(cache_ckv, cache_krope, w_ukv)


_EXP2_SCALE = (QK_NOPE + QK_ROPE) ** -0.5 * LOG2E


def _attn_prompt_kernel(q_ref, k_ref, v_ref, gp_ref, o_ref):
    for b in range(q_ref.shape[0] // SEQ):
        rows = slice(b * SEQ, (b + 1) * SEQ)
        for hd in range(MLA_HEADS):
            ks = slice(hd * HEAD_SLOT, (hd + 1) * HEAD_SLOT)
            vs = slice(hd * LANE, (hd + 1) * LANE)
            s = lax.dot_general(q_ref[rows, ks], k_ref[rows, ks], _NT, preferred_element_type=F32)
            e = jnp.exp2(s - jnp.max(s, axis=-1, keepdims=True))
            p = (e * (1.0 / jnp.sum(e, axis=-1, keepdims=True))).astype(BF16)
            o = jnp.dot(p, v_ref[rows, vs], preferred_element_type=F32)
            o_ref[rows, vs] = (gp_ref[rows, vs] * o).astype(BF16)


def _attn_prompt(q, k, v, z):
    tm = 4 * SEQ
    return pl.pallas_call(
        _attn_prompt_kernel,
        grid=(N_P // tm,),
        in_specs=[pl.BlockSpec((tm, MLA_HEADS * HEAD_SLOT), lambda i: (i, 0)),
                  pl.BlockSpec((tm, MLA_HEADS * HEAD_SLOT), lambda i: (i, 0)),
                  pl.BlockSpec((tm, MLA_W), lambda i: (i, 0)),
                  pl.BlockSpec((tm, Z_BLK), lambda i: (i, ZB_GP_MLA))],
        out_specs=pl.BlockSpec((tm, MLA_W), lambda i: (i, 0)),
        out_shape=jax.ShapeDtypeStruct((N_P, MLA_W), BF16),
        compiler_params=_params(1),
        name="attn_prompt",
    )(q, k, v, z)


def _attn_sample_kernel(q_ref, k_ref, v_ref, kc_ref, vc_ref, gp_ref, o_ref, *, kchunk, heads):
    def lane_fold(x):
        out = x[:, :LANE]
        for c0 in range(LANE, x.shape[1], LANE):
            out = out + x[:, c0:c0 + LANE]
        return out

    for hd in range(heads):
        ks = slice(hd * HEAD_SLOT, (hd + 1) * HEAD_SLOT)
        vs = slice(hd * LANE, (hd + 1) * LANE)
        q = q_ref[:, ks]
        chunks = [(kc_ref[0, 0, :, ks], vc_ref[0, 0, :, vs])]
        chunks += [(k_ref[r:r + kchunk, ks], v_ref[r:r + kchunk, vs]) for r in range(0, DEC_SEQ, kchunk)]
        m = l = acc = None
        for k, v in chunks:
            s = lax.dot_general(q, k, _NT, preferred_element_type=F32)
            m_chunk = jnp.max(s, axis=-1, keepdims=True)
            if m is None:
                m = m_chunk
                e = jnp.exp2(s - m)
                l = lane_fold(e)
                acc = jnp.dot(e.astype(BF16), v, preferred_element_type=F32)
            else:
                m_new = jnp.maximum(m, m_chunk)
                alpha = jnp.exp2(m - m_new)
                e = jnp.exp2(s - m_new)
                l = alpha * l + lane_fold(e)
                acc = alpha * acc + jnp.dot(e.astype(BF16), v, preferred_element_type=F32)
                m = m_new
        inv = 1.0 / jnp.sum(l, axis=-1, keepdims=True)
        o_ref[:, vs] = (gp_ref[:, vs] * (acc * inv)).astype(BF16)


def _attn_sample(q, k, v, kc, vc, z, l):
    tq = 512
    kchunk = PAST_LEN
    heads = 4
    tiles_per_seq = DEC_SEQ // tq
    z_tile0 = N_P // tq
    gp_blk0 = ZB_GP_MLA * Z_BLK // (heads * LANE)
    return pl.pallas_call(
        functools.partial(_attn_sample_kernel, kchunk=kchunk, heads=heads),
        grid=(DEC_BATCH, MLA_HEADS // heads, tiles_per_seq),
        in_specs=[pl.BlockSpec((tq, heads * HEAD_SLOT), lambda b, hb, t: (b * tiles_per_seq + t, hb)),
                  pl.BlockSpec((DEC_SEQ, heads * HEAD_SLOT), lambda b, hb, t: (b, hb)),
                  pl.BlockSpec((DEC_SEQ, heads * LANE), lambda b, hb, t: (b, hb)),
                  pl.BlockSpec((1, 1, PAST_LEN, heads * HEAD_SLOT), lambda b, hb, t: (l, b, 0, hb)),
                  pl.BlockSpec((1, 1, PAST_LEN, heads * LANE), lambda b, hb, t: (l, b, 0, hb)),
                  pl.BlockSpec((tq, heads * LANE),
                               lambda b, hb, t: (z_tile0 + b * tiles_per_seq + t, gp_blk0 + hb))],
        out_specs=pl.BlockSpec((tq, heads * LANE), lambda b, hb, t: (b * tiles_per_seq + t, hb)),
        out_shape=jax.ShapeDtypeStruct((N_S, MLA_W), BF16),
        compiler_params=_params(3),
        name="attn_sample",
    )(q, k, v, kc, vc, z)


def _sgu_kernel(u_ref, v_ref, gp_ref, g_ref, w_ref, b_ref, o_ref, *, chunks):
    for c in range(chunks):
        rows = slice(c * SGU_CHUNK, (c + 1) * SGU_CHUNK)
        vn = _rms(v_ref[rows, :], g_ref[...]).astype(BF16)
        for g in range(SGU_GROUPS):
            cols = slice(g * LANE, (g + 1) * LANE)
            s = jnp.dot(w_ref[0, g], vn[:, cols], preferred_element_type=F32) + b_ref[0, :, cols]
            o_ref[rows, cols] = (gp_ref[rows, cols] * (u_ref[rows, cols] * s)).astype(BF16)


def _sgu(z, g_sgu_l, w_sgu, b_sgu_exp, l):
    tm = 1024
    return pl.pallas_call(
        functools.partial(_sgu_kernel, chunks=tm // SGU_CHUNK),
        grid=(N_TOK // tm,),
        in_specs=[pl.BlockSpec((tm, Z_BLK), lambda i: (i, ZB_U)),
                  pl.BlockSpec((tm, Z_BLK), lambda i: (i, ZB_VS)),
                  pl.BlockSpec((tm, Z_BLK), lambda i: (i, ZB_GP_SGU)),
                  pl.BlockSpec((1, SGU_W), lambda i: (0, 0)),
                  pl.BlockSpec((1, SGU_GROUPS, SGU_CHUNK, SGU_CHUNK), lambda i: (l, 0, 0, 0)),
                  pl.BlockSpec((1, SGU_CHUNK, SGU_W), lambda i: (l, 0, 0))],
        out_specs=pl.BlockSpec((tm, SGU_W), lambda i: (i, 0)),
        out_shape=jax.ShapeDtypeStruct((N_TOK, SGU_W), BF16),
        compiler_params=_params(1),
        name="sgu",
    )(z, z, z, g_sgu_l, w_sgu, b_sgu_exp)


def _log_sigmoid(x):
    return -(jnp.maximum(-x, 0.0) + jnp.log1p(jnp.exp(-jnp.abs(x))))


def _ret_kernel(q_ref, k_ref, v_ref, gp_ref, dec_ref, g_ref, *rest, block, n_chunks, n_seq, heads, has_state,
                emit_state):
    rest = list(rest)
    r0_ref = rest.pop(0) if has_state else None
    rout_ref = rest.pop() if emit_state else None
    o_ref = rest.pop()
    C = block
    shape = (RET_DK, RET_DV)
    ii = lax.broadcasted_iota(jnp.int32, (C, C), 0).astype(F32)
    jj = lax.broadcasted_iota(jnp.int32, (C, C), 1).astype(F32)
    ri = lax.broadcasted_iota(jnp.int32, (C, RET_DV), 0).astype(F32)
    cj = lax.broadcasted_iota(jnp.int32, (RET_DK, C), 1).astype(F32)
    lower = ii >= jj
    k_scale = RET_DK ** -0.5

    for hd in range(heads):
        cols = slice(hd * LANE, (hd + 1) * LANE)
        lg_f8 = _log_sigmoid(dec_ref[0, 0, hd])
        lg_b8 = _log_sigmoid(dec_ref[0, 1, hd])

        def spread(x8, rows, width):
            return jnp.tile(x8, (rows // 8, width // LANE))

        dec = jnp.where(lower, jnp.exp(jnp.where(lower, ii - jj, 0.0) * spread(lg_f8, C, C)),
                        jnp.exp(jnp.where(lower, 0.0, jj - ii) * spread(lg_b8, C, C)))
        xi_f = jnp.exp((ri + 1.0) * spread(lg_f8, C, RET_DV))
        xi_b = jnp.exp(((C - 1.0 - ri) + 1.0) * spread(lg_b8, C, RET_DV))
        zeta_f = jnp.exp((C - 1.0 - cj) * spread(lg_f8, RET_DK, C))
        zeta_b = jnp.exp((C - 1.0 - (C - 1.0 - cj)) * spread(lg_b8, RET_DK, C))
        gc_f = jnp.exp(C * spread(lg_f8, RET_DK, RET_DV))
        gc_b = jnp.exp(C * spread(lg_b8, RET_DK, RET_DV))

        for sq in range(n_seq):
            def rows_of(c, sq=sq):
                return slice(sq * n_chunks * C + c * C, sq * n_chunks * C + (c + 1) * C)

            kvs = []
            for c in range(n_chunks):
                k_t = jnp.transpose(k_ref[rows_of(c), cols] * k_scale)
                kz = jnp.concatenate([k_t * zeta_f, k_t * zeta_b], axis=0).astype(BF16)
                kvs.append(jnp.dot(kz, v_ref[rows_of(c), cols].astype(BF16), preferred_element_type=F32))

            rfs, rbs = [None] * n_chunks, [None] * n_chunks
            r_f = r0_ref[sq, 0, 0, hd] if has_state else jnp.zeros(shape, F32)
            for c in range(n_chunks):
                rfs[c] = r_f.astype(BF16)
                r_f = gc_f * r_f + kvs[c][:RET_DK, :]
            r_b = r0_ref[sq, 0, 1, hd] if has_state else jnp.zeros(shape, F32)
            for c in reversed(range(n_chunks)):
                rbs[c] = r_b.astype(BF16)
                r_b = gc_b * r_b + kvs[c][RET_DK:, :]
            if emit_state:
                for s in range(rout_ref.shape[1]):
                    rout_ref[sq, s, 0, hd] = r_f if s == 0 else jnp.zeros(shape, F32)
                    rout_ref[sq, s, 1, hd] = r_b if s == 0 else jnp.zeros(shape, F32)

            for c in range(n_chunks):
                q = q_ref[rows_of(c), cols].astype(BF16)
                k = (k_ref[rows_of(c), cols] * k_scale).astype(BF16)
                att = lax.dot_general(q, k, _NT, preferred_element_type=F32) * dec
                o = jnp.dot(att.astype(BF16), v_ref[rows_of(c), cols].astype(BF16), preferred_element_type=F32)
                use_f = has_state or c > 0
                use_b = has_state or c < n_chunks - 1
                if use_f and use_b:
                    inter = jnp.dot(q, jnp.concatenate([rfs[c], rbs[c]], axis=1), preferred_element_type=F32)
                    o = o + inter[:, :RET_DV] * xi_f + inter[:, RET_DV:] * xi_b
                elif use_f:
                    o = o + jnp.dot(q, rfs[c], preferred_element_type=F32) * xi_f
                elif use_b:
                    o = o + jnp.dot(q, rbs[c], preferred_element_type=F32) * xi_b
                mu = jnp.mean(o, axis=-1, keepdims=True)
                var = jnp.mean(jnp.square(o - mu), axis=-1, keepdims=True)
                y = (o - mu) * lax.rsqrt(var + EPS) * g_ref[:, cols]
                o_ref[rows_of(c), cols] = (gp_ref[rows_of(c), cols] * y).astype(BF16)


def _retention(z, decay_b, g_ret_l, state_ret, l, group, stack=None):
    seq = SEQ if group == 0 else DEC_SEQ
    nb = BATCH if group == 0 else DEC_BATCH
    heads = RET_HEADS if group == 0 else 2
    n_seq = 2 if group == 0 else 1
    hblk = RET_HEADS // heads
    b0 = 0 if group == 0 else N_P // (n_seq * seq)
    has_state = group == 1
    emit_state = group == 0
    block = 2 * RET_CHUNK
    n_chunks = seq // block
    width = heads * LANE

    def zspec(zb):
        return pl.BlockSpec((n_seq * seq, width), lambda b, hb: (b0 + b, zb * Z_BLK // width + hb))

    in_specs = [zspec(ZB_RQ), zspec(ZB_RK), zspec(ZB_RV), zspec(ZB_GP_RET),
                pl.BlockSpec((1, 2, heads, 8, LANE), lambda b, hb: (l, 0, hb, 0, 0)),
                pl.BlockSpec((1, width), lambda b, hb: (0, hb))]
    args = [z, z, z, z, decay_b, g_ret_l]
    state_block = (n_seq, 1, 2, heads, RET_DK, RET_DV)
    if has_state:
        in_specs.append(pl.BlockSpec(state_block, lambda b, hb: (b, l, 0, hb, 0, 0)))
        args.append(state_ret)
    out_specs = [pl.BlockSpec((n_seq * seq, width), lambda b, hb: (b, hb))]
    out_shape = [jax.ShapeDtypeStruct((nb * seq, RET_W), BF16)]
    aliases = {}
    if emit_state:
        assert (l == 0) == (stack is None)
        slices = DEPTH if stack is None else 1
        out_specs.append(pl.BlockSpec((n_seq, slices) + state_block[2:], lambda b, hb: (b, l, 0, hb, 0, 0)))
        out_shape.append(jax.ShapeDtypeStruct((nb, DEPTH, 2, RET_HEADS, RET_DK, RET_DV), F32))
        if stack is not None:
            aliases = {len(args): 1}
            in_specs.append(pl.BlockSpec(memory_space=pl.ANY))
            args.append(stack)
    return pl.pallas_call(
        functools.partial(_ret_kernel, block=block, n_chunks=n_chunks, n_seq=n_seq, heads=heads,
                          has_state=has_state, emit_state=emit_state),
        grid=(nb // n_seq, hblk),
        in_specs=in_specs,
        out_specs=out_specs,
        out_shape=out_shape,
        input_output_aliases=aliases,
        compiler_params=_params(2),
        name="ret_sample" if has_state else "ret_prompt",
    )(*args)


def _tail_kernel(ap_m, as_m, a_sgu, ap_r, as_r, m_ref, wm_ref, ws_ref, wr_ref, wo_ref,
                 gpost_ref, mod_ref, *rest, tiles_p, first, last):
    i = pl.program_id(0)
    is_p = i < tiles_p
    rest = list(rest)
    if first:
        x = jnp.where(is_p, rest.pop(0)[...], rest.pop(0)[...])
    else:
        x = rest.pop(0)[...]
    a_mla = jnp.where(is_p, ap_m[...], as_m[...])
    a_ret = jnp.where(is_p, ap_r[...], as_r[...])
    y_mla = jnp.dot(a_mla, wm_ref[0], preferred_element_type=F32)
    y = m_ref[:, :D_MODEL] * y_mla
    y_sgu = jnp.dot(a_sgu[...], ws_ref[0], preferred_element_type=F32)
    y = y + m_ref[:, D_MODEL:2 * D_MODEL] * y_sgu
    y_ret = jnp.dot(a_ret, wr_ref[0], preferred_element_type=F32)
    y = y + m_ref[:, 2 * D_MODEL:] * y_ret
    out = jnp.dot(y.astype(BF16), wo_ref[0], preferred_element_type=F32)
    x_new = x + mod_ref[0, 2:3, :] * _rms(out, gpost_ref[...])
    if last:
        yp_ref, ys_ref = rest

        @pl.when(is_p)
        def _():
            yp_ref[...] = x_new

        @pl.when(jnp.logical_not(is_p))
        def _():
            ys_ref[...] = x_new
    else:
        gpre_ref, modn_ref, x_out, h_out = rest
        x_out[...] = x_new
        h_out[...] = _prenorm_value(x_new, gpre_ref[...], modn_ref)


def _tail(a_mla_p, a_mla_s, a_sgu, a_ret_p, a_ret_s, z, w_br_mla, w_br_sgu, w_br_ret, w_out, xs, g_post_l,
          g_pre_next, mod3, l):
    tm = 256
    tiles_p = N_P // tm
    last = g_pre_next is None
    first = len(xs) == 2

    def p_spec():
        return pl.BlockSpec((tm, 1024), lambda i: (jnp.minimum(i, tiles_p - 1), 0))

    def s_spec():
        return pl.BlockSpec((tm, 1024), lambda i: (jnp.maximum(i - tiles_p, 0), 0))

    def w_spec(rows):
        return pl.BlockSpec((1, rows, D_MODEL), lambda i: (l, 0, 0), pipeline_mode=pl.Buffered(1))

    def row_spec():
        return pl.BlockSpec((tm, D_MODEL), lambda i: (i, 0))

    in_specs = [p_spec(), s_spec(), pl.BlockSpec((tm, 1024), lambda i: (i, 0)), p_spec(), s_spec()]
    in_specs += [pl.BlockSpec((pl.Element(tm), pl.Element(N_BRANCH * D_MODEL)),
                              lambda i: (pl.multiple_of(i * tm, tm), ZB_MERGE * Z_BLK))]
    in_specs += [w_spec(1024), w_spec(1024), w_spec(1024), w_spec(D_MODEL),
                 pl.BlockSpec((1, D_MODEL), lambda i: (0, 0)), _mod_spec(l, tm)]
    args = [a_mla_p, a_mla_s, a_sgu, a_ret_p, a_ret_s, z]
    args += [w_br_mla, w_br_sgu, w_br_ret, w_out, g_post_l, mod3]
    if first:
        in_specs += [pl.BlockSpec((tm, D_MODEL), lambda i: (jnp.minimum(i, tiles_p - 1), 0)),
                     pl.BlockSpec((tm, D_MODEL), lambda i: (jnp.maximum(i - tiles_p, 0), 0))]
    else:
        in_specs += [row_spec()]
    args += list(xs)
    if last:
        out_specs = [pl.BlockSpec((tm, D_MODEL), lambda i: (jnp.minimum(i, tiles_p - 1), 0)),
                     pl.BlockSpec((tm, D_MODEL), lambda i: (jnp.maximum(i - tiles_p, 0), 0))]
        out_shape = [jax.ShapeDtypeStruct((N_P, D_MODEL), F32), jax.ShapeDtypeStruct((N_S, D_MODEL), F32)]
    else:
        in_specs += [pl.BlockSpec((1, D_MODEL), lambda i: (0, 0)), _mod_spec(l + 1, tm)]
        args += [g_pre_next, mod3]
        out_specs = [row_spec(), row_spec()]
        out_shape = [jax.ShapeDtypeStruct((N_TOK, D_MODEL), F32), jax.ShapeDtypeStruct((N_TOK, D_MODEL), BF16)]
    return pl.pallas_call(
        functools.partial(_tail_kernel, tiles_p=tiles_p, first=first, last=last),
        grid=(N_TOK // tm,),
        in_specs=in_specs,
        out_specs=out_specs,
        out_shape=out_shape,
        compiler_params=_params(1),
        name="tail_last" if last else "tail",
    )(*args)


def _rope_tables():
    f32 = np.float32
    pos = np.arange(DEC_SEQ)
    row = (pos // GRID_W).astype(f32)
    col = (pos % GRID_W).astype(f32)
    n_freq = QK_ROPE // 4
    inv = f32(ROPE_BASE) ** (-np.arange(n_freq, dtype=f32) / f32(n_freq))
    ang_r = row[:, None] * inv
    ang_c = col[:, None] * inv
    zeros = np.zeros((DEC_SEQ, n_freq), f32)
    pad = np.zeros((DEC_SEQ, LANE - QK_ROPE), f32)
    cr, sr, cc, sc = np.cos(ang_r), np.sin(ang_r), np.cos(ang_c), np.sin(ang_c)
    c = np.concatenate([cr, cr, cc, cc, pad], axis=1)
    s1 = np.concatenate([-sr, zeros, -sc, zeros, pad], axis=1)
    s2 = np.concatenate([zeros, sr, zeros, sc, pad], axis=1)
    return tuple(jnp.asarray(t.astype(f32)) for t in (c, s1, s2))


def kernel(x_prompt, x_sample, cache_ckv, cache_krope, state_ret, c, c_ctx, w_mod, b_mod, g_pre, g_post, w_in,
           g_q, g_kv, w_uq, w_ukv, g_sgu, w_sgu, b_sgu, ret_decay, g_ret, w_br_mla, w_br_sgu, w_br_ret, w_out):
    cond8 = jnp.concatenate([c_ctx[None, :], c, jnp.zeros((COND_ROWS - 1 - DEC_BATCH, D_MODEL), F32)], axis=0)
    mod3 = _modulation(cond8, w_mod, b_mod).reshape(DEPTH * COND_ROWS, 3, D_MODEL)
    w_in_t = jnp.swapaxes(w_in, 1, 2)
    w_kr = jnp.pad(w_in_t[:, KR_COL:KR_COL + QK_ROPE, :], ((0, 0), (0, LANE - QK_ROPE), (0, 0))).astype(BF16)
    w_uq_pad = jnp.pad(w_uq.reshape(DEPTH, Q_LORA, MLA_HEADS, QK_NOPE + QK_ROPE),
                       ((0, 0), (0, 0), (0, 0), (0, HEAD_SLOT - QK_NOPE - QK_ROPE))
                       ).reshape(DEPTH, Q_LORA, MLA_HEADS * HEAD_SLOT).astype(BF16)
    w_ukv_bf = w_ukv.astype(BF16)
    w_sgu_bf = w_sgu.astype(BF16)
    w_br = [w.astype(BF16) for w in (w_br_mla, w_br_sgu, w_br_ret)]
    w_out_bf = w_out.astype(BF16)
    b_sgu_exp = jnp.repeat(b_sgu, LANE, axis=2)
    decay_b = jnp.broadcast_to(ret_decay[:, :, :, None, None], (DEPTH, 2, RET_HEADS, 8, LANE))
    rope_tabs = _rope_tables()
    kc, vc = _ctxkv(cache_ckv, cache_krope, w_ukv_bf)

    xs = (x_prompt.reshape(N_P, D_MODEL), x_sample.reshape(N_S, D_MODEL))
    h = _prenorm(*xs, g_pre[0][None, :], mod3)
    cache_stacks = ret_stack = None
    for l in range(DEPTH):
        z = _inproj(h, w_in_t, l)
        q_p, k_p, v_p, *cache_stacks = _qkv(z, h, w_kr, w_uq_pad, w_ukv_bf, g_q[l][None, :], g_kv[l][None, :], None,
                                            l, 0, cache_stacks)
        a_mla_p = _attn_prompt(q_p, k_p, v_p, z)
        q_s, k_s, v_s = _qkv(z, h, w_kr, w_uq_pad, w_ukv_bf, g_q[l][None, :], g_kv[l][None, :], rope_tabs, l, 1)
        a_mla_s = _attn_sample(q_s, k_s, v_s, kc, vc, z, l)
        a_sgu = _sgu(z, g_sgu[l][None, :], w_sgu_bf, b_sgu_exp, l)
        a_ret_p, ret_stack = _retention(z, decay_b, g_ret[l][None, :], None, l, 0, ret_stack)
        (a_ret_s,) = _retention(z, decay_b, g_ret[l][None, :], state_ret, l, 1)
        g_pre_next = g_pre[l + 1][None, :] if l + 1 < DEPTH else None
        outs = _tail(a_mla_p, a_mla_s, a_sgu, a_ret_p, a_ret_s, z, *w_br, w_out_bf, xs, g_post[l][None, :],
                     g_pre_next, mod3, l)
        if l + 1 < DEPTH:
            x, h = outs
            xs = (x,)
        else:
            y_p, y_s = outs
    return (y_p.reshape(BATCH, SEQ, D_MODEL), y_s.reshape(DEC_BATCH, DEC_SEQ, D_MODEL), *cache_stacks, ret_stack)
```

```python
import functools

import jax
import jax.numpy as jnp
import numpy as np
from jax import lax
from jax.experimental import pallas as pl
from jax.experimental.pallas import tpu as pltpu

F32 = jnp.float32
BF16 = jnp.bfloat16

D_MODEL = 2048
BATCH = 16
SEQ = 256
DEPTH = 4
DEC_BATCH = 2
DEC_SEQ = 2048
PAST_LEN = 256
GRID_W = 64
EPS = 1e-6
ROPE_BASE = 10000.0

MLA_HEADS = 8
QK_NOPE = 128
QK_ROPE = 64
V_HEAD = 128
Q_LORA = 512
KV_LORA = 512
MLA_W = MLA_HEADS * V_HEAD
SGU_CHUNK = 128
SGU_GROUPS = 8
SGU_W = 1024
RET_HEADS = 8
RET_DK = 128
RET_DV = 128
RET_W = RET_HEADS * RET_DV
RET_CHUNK = 128
N_BRANCH = 3

LANE = 128
N_P = BATCH * SEQ
N_S = DEC_BATCH * DEC_SEQ
N_TOK = N_P + N_S
COND_ROWS = 8
HEAD_SLOT = 2 * LANE
KR_COL = Q_LORA + KV_LORA
Z_BLK = 1024
N_ZBLK = 15
ZB_GP_MLA, ZB_U, ZB_VS, ZB_GP_SGU, ZB_RQ, ZB_RK, ZB_RV, ZB_GP_RET, ZB_MERGE = 1, 2, 3, 4, 5, 6, 7, 8, 9
VMEM_LIMIT = 56 * 1024 * 1024
TAIL_STAGE_ROWS = 64
LOG2E = 1.4426950408889634

_NT = (((1,), (1,)), ((), ()))


def _params(n_axes, vmem=VMEM_LIMIT):
    return pltpu.CompilerParams(dimension_semantics=("arbitrary",) * n_axes, vmem_limit_bytes=vmem)


def _rms(x, g):
    return x * lax.rsqrt(jnp.mean(x * x, axis=-1, keepdims=True) + EPS) * g


def _silu(x):
    return x * jax.nn.sigmoid(x)


def _cond_row(tile, tm):
    return jnp.where(tile < N_P // tm, 0, 1 + (tile - N_P // tm) // (DEC_SEQ // tm))


def _mod_spec(l, tm):
    return pl.BlockSpec((1, 3, D_MODEL), lambda i: (l * COND_ROWS + _cond_row(i, tm), 0, 0))


def _prenorm_value(x, g, mod_ref):
    return (_rms(x, g) * (1.0 + mod_ref[0, 1:2, :]) + mod_ref[0, 0:1, :]).astype(BF16)


def _mod_kernel(cond_ref, w_ref, b_ref, o_ref):
    s = _silu(cond_ref[...]).astype(BF16)
    o_ref[0] = jnp.dot(s, w_ref[0].astype(BF16), preferred_element_type=F32) + b_ref[0]


def _modulation(cond8, w_mod, b_mod):
    tn = 2048
    return pl.pallas_call(
        _mod_kernel,
        grid=(DEPTH, 3 * D_MODEL // tn),
        in_specs=[pl.BlockSpec((COND_ROWS, D_MODEL), lambda l, n: (0, 0)),
                  pl.BlockSpec((1, D_MODEL, tn), lambda l, n: (l, 0, n)),
                  pl.BlockSpec((1, 1, tn), lambda l, n: (l, 0, n))],
        out_specs=pl.BlockSpec((1, COND_ROWS, tn), lambda l, n: (l, 0, n)),
        out_shape=jax.ShapeDtypeStruct((DEPTH, COND_ROWS, 3 * D_MODEL), F32),
        compiler_params=_params(2),
        name="modulation",
    )(cond8, w_mod, b_mod.reshape(DEPTH, 1, 3 * D_MODEL))


def _prenorm_kernel(xp_ref, xs_ref, g_ref, mod_ref, h_ref, *, tiles_p):
    x = jnp.where(pl.program_id(0) < tiles_p, xp_ref[...], xs_ref[...])
    h_ref[...] = _prenorm_value(x, g_ref[...], mod_ref)


def _prenorm(x_p, x_s, g_pre_l, mod3):
    tm = 512
    tiles_p = N_P // tm
    return pl.pallas_call(
        functools.partial(_prenorm_kernel, tiles_p=tiles_p),
        grid=(N_TOK // tm,),
        in_specs=[pl.BlockSpec((tm, D_MODEL), lambda i: (jnp.minimum(i, tiles_p - 1), 0)),
                  pl.BlockSpec((tm, D_MODEL), lambda i: (jnp.maximum(i - tiles_p, 0), 0)),
                  pl.BlockSpec((1, D_MODEL), lambda i: (0, 0)),
                  _mod_spec(0, tm)],
        out_specs=pl.BlockSpec((tm, D_MODEL), lambda i: (i, 0)),
        out_shape=jax.ShapeDtypeStruct((N_TOK, D_MODEL), BF16),
        compiler_params=_params(1),
        name="prenorm",
    )(x_p, x_s, g_pre_l, mod3)


def _inproj_kernel(h_ref, w_ref, o_ref, wbf_ref):
    j = pl.program_id(0)
    first_tile = pl.program_id(1) == 0
    is_silu = jnp.logical_or(jnp.logical_or(j == ZB_GP_MLA, j == ZB_GP_SGU), j == ZB_GP_RET)
    is_act = jnp.logical_or(is_silu, j >= ZB_MERGE)

    def weights(cast):
        if not cast:
            return wbf_ref[...]
        w = w_ref[0].astype(BF16)
        wbf_ref[...] = w
        return w

    for cast in (True, False):
        on_tile = first_tile if cast else jnp.logical_not(first_tile)

        @pl.when(jnp.logical_and(on_tile, is_act))
        def _():
            acc = lax.dot_general(h_ref[...], weights(cast), _NT, preferred_element_type=F32)
            sig = 0.5 * jnp.tanh(0.5 * acc) + 0.5
            o_ref[...] = jnp.where(is_silu, acc * sig, sig)

        @pl.when(jnp.logical_and(on_tile, jnp.logical_not(is_act)))
        def _():
            o_ref[...] = lax.dot_general(h_ref[...], weights(cast), _NT, preferred_element_type=F32)


def _inproj(h, w_in_t, l):
    tm = 1024
    return pl.pallas_call(
        _inproj_kernel,
        grid=(N_ZBLK, N_TOK // tm),
        in_specs=[pl.BlockSpec((tm, D_MODEL), lambda j, i: (i, 0)),
                  pl.BlockSpec((pl.Element(1), pl.Element(Z_BLK), pl.Element(D_MODEL)),
                               lambda j, i: (l, pl.multiple_of(Z_BLK * j + jnp.where(j > 0, QK_ROPE, 0), QK_ROPE),
                                             0))],
        out_specs=pl.BlockSpec((tm, Z_BLK), lambda j, i: (i, j)),
        out_shape=jax.ShapeDtypeStruct((N_TOK, N_ZBLK * Z_BLK), F32),
        scratch_shapes=[pltpu.VMEM((Z_BLK, D_MODEL), BF16)],
        compiler_params=_params(2),
        name="inproj",
    )(h, w_in_t)


def _rope(x, c, s1, s2):
    return x * c + pltpu.roll(x, LANE - QK_ROPE // 4, 1) * s1 + pltpu.roll(x, QK_ROPE // 4, 1) * s2


def _qkv_kernel(z_ref, h_ref, wkr_ref, wuq_ref, wukv_ref, gq_ref, gkv_ref, *rest, rope):
    if rope:
        c_ref, s1_ref, s2_ref, q_ref, k_ref, v_ref = rest
    else:
        q_ref, k_ref, v_ref, ckvn_ref, kr_ref = rest[-5:]
    cq = z_ref[:, :Q_LORA]
    ckv = z_ref[:, Q_LORA:]
    q = jnp.dot(_rms(cq, gq_ref[...]).astype(BF16), wuq_ref[0], preferred_element_type=F32) * _EXP2_SCALE
    ckvn = _rms(ckv, gkv_ref[...])
    kv = jnp.dot(ckvn.astype(BF16), wukv_ref[0], preferred_element_type=F32)
    kr = lax.dot_general(h_ref[...], wkr_ref[0], _NT, preferred_element_type=F32)
    if rope:
        c, s1, s2 = c_ref[...], s1_ref[...], s2_ref[...]
        kr = _rope(kr, c, s1, s2)
    else:
        for b in range(ckvn_ref.shape[0]):
            rows = slice(b * SEQ, (b + 1) * SEQ)
            for s in range(ckvn_ref.shape[1]):
                ckvn_ref[b, s] = ckvn[rows, :] if s == 0 else jnp.zeros((SEQ, KV_LORA), F32)
                kr_ref[b, s] = kr[rows, :QK_ROPE] if s == 0 else jnp.zeros((SEQ, QK_ROPE), F32)
    kr_bf = kr.astype(BF16)
    for hd in range(MLA_HEADS):
        lo = hd * HEAD_SLOT
        q_ref[:, lo:lo + LANE] = q[:, lo:lo + LANE].astype(BF16)
        qr = q[:, lo + LANE:lo + HEAD_SLOT]
        if rope:
            qr = _rope(qr, c, s1, s2)
        q_ref[:, lo + LANE:lo + HEAD_SLOT] = qr.astype(BF16)
        k_ref[:, lo:lo + LANE] = kv[:, lo:lo + LANE].astype(BF16)
        k_ref[:, lo + LANE:lo + HEAD_SLOT] = kr_bf
        v_ref[:, hd * LANE:(hd + 1) * LANE] = kv[:, lo + LANE:lo + HEAD_SLOT].astype(BF16)


def _qkv(z, h, w_kr, w_uq_pad, w_ukv, g_q_l, g_kv_l, rope_tabs, l, group, stacks=None):
    tm = 512
    aliases = {}
    n_tok = N_P if group == 0 else N_S
    t0 = 0 if group == 0 else N_P // tm
    rope = group == 1
    in_specs = [pl.BlockSpec((tm, Z_BLK), lambda i: (i + t0, 0)),
                pl.BlockSpec((tm, D_MODEL), lambda i: (i + t0, 0)),
                pl.BlockSpec((1, LANE, D_MODEL), lambda i: (l, 0, 0)),
                pl.BlockSpec((1, Q_LORA, MLA_HEADS * HEAD_SLOT), lambda i: (l, 0, 0)),
                pl.BlockSpec((1, KV_LORA, MLA_HEADS * HEAD_SLOT), lambda i: (l, 0, 0)),
                pl.BlockSpec((1, Q_LORA), lambda i: (0, 0)),
                pl.BlockSpec((1, KV_LORA), lambda i: (0, 0))]
    args = [z, h, w_kr, w_uq_pad, w_ukv, g_q_l, g_kv_l]
    out_specs = [pl.BlockSpec((tm, MLA_HEADS * HEAD_SLOT), lambda i: (i, 0)),
                 pl.BlockSpec((tm, MLA_HEADS * HEAD_SLOT), lambda i: (i, 0)),
                 pl.BlockSpec((tm, MLA_W), lambda i: (i, 0))]
    out_shape = [jax.ShapeDtypeStruct((n_tok, MLA_HEADS * HEAD_SLOT), BF16),
                 jax.ShapeDtypeStruct((n_tok, MLA_HEADS * HEAD_SLOT), BF16),
                 jax.ShapeDtypeStruct((n_tok, MLA_W), BF16)]
    if rope:
        tiles_per_seq = DEC_SEQ // tm
        in_specs += [pl.BlockSpec((tm, LANE), lambda i: (i % tiles_per_seq, 0))] * 3
        args += list(rope_tabs)
    else:
        assert (l == 0) == (stacks is None)
        slices = DEPTH if stacks is None else 1
        out_specs += [pl.BlockSpec((tm // SEQ, slices, SEQ, KV_LORA), lambda i: (i, l, 0, 0)),
                      pl.BlockSpec((tm // SEQ, slices, SEQ, QK_ROPE), lambda i: (i, l, 0, 0))]
        out_shape += [jax.ShapeDtypeStruct((BATCH, DEPTH, SEQ, KV_LORA), F32),
                      jax.ShapeDtypeStruct((BATCH, DEPTH, SEQ, QK_ROPE), F32)]
        if stacks is not None:
            aliases = {len(args): 3, len(args) + 1: 4}
            in_specs += [pl.BlockSpec(memory_space=pl.ANY)] * 2
            args += list(stacks)
    return pl.pallas_call(
        functools.partial(_qkv_kernel, rope=rope),
        grid=(n_tok // tm,),
        in_specs=in_specs,
        out_specs=out_specs,
        out_shape=out_shape,
        input_output_aliases=aliases,
        compiler_params=_params(1),
        name="qkv_sample" if rope else "qkv_prompt",
    )(*args)


def _ctxkv_kernel(ckv_ref, kr_ref, w_ref, k_ref, v_ref):
    kv = jnp.dot(ckv_ref[0, 0].astype(BF16), w_ref[0], preferred_element_type=F32)
    kr = jnp.concatenate([kr_ref[0, 0], jnp.zeros((PAST_LEN, LANE - QK_ROPE), F32)], axis=1).astype(BF16)
    for hd in range(MLA_HEADS):
        lo = hd * HEAD_SLOT
        k_ref[0, 0, :, lo:lo + LANE] = kv[:, lo:lo + LANE].astype(BF16)
        k_ref[0, 0, :, lo + LANE:lo + HEAD_SLOT] = kr
        v_ref[0, 0, :, hd * LANE:(hd + 1) * LANE] = kv[:, lo + LANE:lo + HEAD_SLOT].astype(BF16)


def _ctxkv(cache_ckv, cache_krope, w_ukv):
    return pl.pallas_call(
        _ctxkv_kernel,
        grid=(DEPTH, DEC_BATCH),
        in_specs=[pl.BlockSpec((1, 1, PAST_LEN, KV_LORA), lambda l, b: (b, l, 0, 0)),
                  pl.BlockSpec((1, 1, PAST_LEN, QK_ROPE), lambda l, b: (b, l, 0, 0)),
                  pl.BlockSpec((1, KV_LORA, MLA_HEADS * HEAD_SLOT), lambda l, b: (l, 0, 0))],
        out_specs=[pl.BlockSpec((1, 1, PAST_LEN, MLA_HEADS * HEAD_SLOT), lambda l, b: (l, b, 0, 0)),
                   pl.BlockSpec((1, 1, PAST_LEN, MLA_W), lambda l, b: (l, b, 0, 0))],
        out_shape=[jax.ShapeDtypeStruct((DEPTH, DEC_BATCH, PAST_LEN, MLA_HEADS * HEAD_SLOT), BF16),
                   jax.ShapeDtypeStruct((DEPTH, DEC_BATCH, PAST_LEN, MLA_W), BF16)],
        compiler_params=_params(2),
        name="ctxkv",
    )(cache_ckv, cache_krope, w_ukv)


_EXP2_SCALE = (QK_NOPE + QK_ROPE) ** -0.5 * LOG2E


def _attn_prompt_kernel(q_ref, k_ref, v_ref, gp_ref, o_ref):
    for b in range(q_ref.shape[0] // SEQ):
        rows = slice(b * SEQ, (b + 1) * SEQ)
        for hd in range(MLA_HEADS):
            ks = slice(hd * HEAD_SLOT, (hd + 1) * HEAD_SLOT)
            vs = slice(hd * LANE, (hd + 1) * LANE)
            s = lax.dot_general(q_ref[rows, ks], k_ref[rows, ks], _NT, preferred_element_type=F32)
            e = jnp.exp2(s - jnp.max(s, axis=-1, keepdims=True))
            p = (e * (1.0 / jnp.sum(e, axis=-1, keepdims=True))).astype(BF16)
            o = jnp.dot(p, v_ref[rows, vs], preferred_element_type=F32)
            o_ref[rows, vs] = (gp_ref[rows, vs] * o).astype(BF16)


def _attn_prompt(q, k, v, z):
    tm = 4 * SEQ
    return pl.pallas_call(
        _attn_prompt_kernel,
        grid=(N_P // tm,),
        in_specs=[pl.BlockSpec((tm, MLA_HEADS * HEAD_SLOT), lambda i: (i, 0)),
                  pl.BlockSpec((tm, MLA_HEADS * HEAD_SLOT), lambda i: (i, 0)),
                  pl.BlockSpec((tm, MLA_W), lambda i: (i, 0)),
                  pl.BlockSpec((tm, Z_BLK), lambda i: (i, ZB_GP_MLA))],
        out_specs=pl.BlockSpec((tm, MLA_W), lambda i: (i, 0)),
        out_shape=jax.ShapeDtypeStruct((N_P, MLA_W), BF16),
        compiler_params=_params(1),
        name="attn_prompt",
    )(q, k, v, z)


def _attn_sample_kernel(q_ref, k_ref, v_ref, kc_ref, vc_ref, gp_ref, o_ref, *, kchunk, heads):
    def lane_fold(x):
        out = x[:, :LANE]
        for c0 in range(LANE, x.shape[1], LANE):
            out = out + x[:, c0:c0 + LANE]
        return out

    for hd in range(heads):
        ks = slice(hd * HEAD_SLOT, (hd + 1) * HEAD_SLOT)
        vs = slice(hd * LANE, (hd + 1) * LANE)
        q = q_ref[:, ks]
        chunks = [(kc_ref[0, 0, :, ks], vc_ref[0, 0, :, vs])]
        chunks += [(k_ref[r:r + kchunk, ks], v_ref[r:r + kchunk, vs]) for r in range(0, DEC_SEQ, kchunk)]
        m = l = acc = None
        for k, v in chunks:
            s = lax.dot_general(q, k, _NT, preferred_element_type=F32)
            m_chunk = jnp.max(s, axis=-1, keepdims=True)
            if m is None:
                m = m_chunk
                e = jnp.exp2(s - m)
                l = lane_fold(e)
                acc = jnp.dot(e.astype(BF16), v, preferred_element_type=F32)
            else:
                m_new = jnp.maximum(m, m_chunk)
                alpha = jnp.exp2(m - m_new)
                e = jnp.exp2(s - m_new)
                l = alpha * l + lane_fold(e)
                acc = alpha * acc + jnp.dot(e.astype(BF16), v, preferred_element_type=F32)
                m = m_new
        inv = 1.0 / jnp.sum(l, axis=-1, keepdims=True)
        o_ref[:, vs] = (gp_ref[:, vs] * (acc * inv)).astype(BF16)


def _attn_sample(q, k, v, kc, vc, z, l):
    tq = 512
    kchunk = PAST_LEN
    heads = 4
    tiles_per_seq = DEC_SEQ // tq
    z_tile0 = N_P // tq
    gp_blk0 = ZB_GP_MLA * Z_BLK // (heads * LANE)
    return pl.pallas_call(
        functools.partial(_attn_sample_kernel, kchunk=kchunk, heads=heads),
        grid=(DEC_BATCH, MLA_HEADS // heads, tiles_per_seq),
        in_specs=[pl.BlockSpec((tq, heads * HEAD_SLOT), lambda b, hb, t: (b * tiles_per_seq + t, hb)),
                  pl.BlockSpec((DEC_SEQ, heads * HEAD_SLOT), lambda b, hb, t: (b, hb)),
                  pl.BlockSpec((DEC_SEQ, heads * LANE), lambda b, hb, t: (b, hb)),
                  pl.BlockSpec((1, 1, PAST_LEN, heads * HEAD_SLOT), lambda b, hb, t: (l, b, 0, hb)),
                  pl.BlockSpec((1, 1, PAST_LEN, heads * LANE), lambda b, hb, t: (l, b, 0, hb)),
                  pl.BlockSpec((tq, heads * LANE),
                               lambda b, hb, t: (z_tile0 + b * tiles_per_seq + t, gp_blk0 + hb))],
        out_specs=pl.BlockSpec((tq, heads * LANE), lambda b, hb, t: (b * tiles_per_seq + t, hb)),
        out_shape=jax.ShapeDtypeStruct((N_S, MLA_W), BF16),
        compiler_params=_params(3),
        name="attn_sample",
    )(q, k, v, kc, vc, z)


def _sgu_kernel(u_ref, v_ref, gp_ref, g_ref, w_ref, b_ref, o_ref, *, chunks):
    for c in range(chunks):
        rows = slice(c * SGU_CHUNK, (c + 1) * SGU_CHUNK)
        vn = _rms(v_ref[rows, :], g_ref[...]).astype(BF16)
        for g in range(SGU_GROUPS):
            cols = slice(g * LANE, (g + 1) * LANE)
            s = jnp.dot(w_ref[0, g], vn[:, cols], preferred_element_type=F32) + b_ref[0, :, cols]
            o_ref[rows, cols] = (gp_ref[rows, cols] * (u_ref[rows, cols] * s)).astype(BF16)


def _sgu(z, g_sgu_l, w_sgu, b_sgu_exp, l):
    tm = 1024
    return pl.pallas_call(
        functools.partial(_sgu_kernel, chunks=tm // SGU_CHUNK),
        grid=(N_TOK // tm,),
        in_specs=[pl.BlockSpec((tm, Z_BLK), lambda i: (i, ZB_U)),
                  pl.BlockSpec((tm, Z_BLK), lambda i: (i, ZB_VS)),
                  pl.BlockSpec((tm, Z_BLK), lambda i: (i, ZB_GP_SGU)),
                  pl.BlockSpec((1, SGU_W), lambda i: (0, 0)),
                  pl.BlockSpec((1, SGU_GROUPS, SGU_CHUNK, SGU_CHUNK), lambda i: (l, 0, 0, 0)),
                  pl.BlockSpec((1, SGU_CHUNK, SGU_W), lambda i: (l, 0, 0))],
        out_specs=pl.BlockSpec((tm, SGU_W), lambda i: (i, 0)),
        out_shape=jax.ShapeDtypeStruct((N_TOK, SGU_W), BF16),
        compiler_params=_params(1),
        name="sgu",
    )(z, z, z, g_sgu_l, w_sgu, b_sgu_exp)


def _log_sigmoid(x):
    return -(jnp.maximum(-x, 0.0) + jnp.log1p(jnp.exp(-jnp.abs(x))))


def _ret_kernel(q_ref, k_ref, v_ref, gp_ref, dec_ref, g_ref, *rest, block, n_chunks, n_seq, heads, has_state,
                emit_state):
    rest = list(rest)
    r0_ref = rest.pop(0) if has_state else None
    rout_ref = rest.pop() if emit_state else None
    o_ref = rest.pop()
    C = block
    shape = (RET_DK, RET_DV)
    ii = lax.broadcasted_iota(jnp.int32, (C, C), 0).astype(F32)
    jj = lax.broadcasted_iota(jnp.int32, (C, C), 1).astype(F32)
    ri = lax.broadcasted_iota(jnp.int32, (C, RET_DV), 0).astype(F32)
    cj = lax.broadcasted_iota(jnp.int32, (RET_DK, C), 1).astype(F32)
    lower = ii >= jj
    k_scale = RET_DK ** -0.5

    for hd in range(heads):
        cols = slice(hd * LANE, (hd + 1) * LANE)
        lg_f8 = _log_sigmoid(dec_ref[0, 0, hd])
        lg_b8 = _log_sigmoid(dec_ref[0, 1, hd])

        def spread(x8, rows, width):
            return jnp.tile(x8, (rows // 8, width // LANE))

        dec = jnp.where(lower, jnp.exp(jnp.where(lower, ii - jj, 0.0) * spread(lg_f8, C, C)),
                        jnp.exp(jnp.where(lower, 0.0, jj - ii) * spread(lg_b8, C, C)))
        xi_f = jnp.exp((ri + 1.0) * spread(lg_f8, C, RET_DV))
        xi_b = jnp.exp(((C - 1.0 - ri) + 1.0) * spread(lg_b8, C, RET_DV))
        zeta_f = jnp.exp((C - 1.0 - cj) * spread(lg_f8, RET_DK, C))
        zeta_b = jnp.exp((C - 1.0 - (C - 1.0 - cj)) * spread(lg_b8, RET_DK, C))
        gc_f = jnp.exp(C * spread(lg_f8, RET_DK, RET_DV))
        gc_b = jnp.exp(C * spread(lg_b8, RET_DK, RET_DV))

        for sq in range(n_seq):
            def rows_of(c, sq=sq):
                return slice(sq * n_chunks * C + c * C, sq * n_chunks * C + (c + 1) * C)

            kvs = []
            for c in range(n_chunks):
                k_t = jnp.transpose(k_ref[rows_of(c), cols] * k_scale)
                kz = jnp.concatenate([k_t * zeta_f, k_t * zeta_b], axis=0).astype(BF16)
                kvs.append(jnp.dot(kz, v_ref[rows_of(c), cols].astype(BF16), preferred_element_type=F32))

            rfs, rbs = [None] * n_chunks, [None] * n_chunks
            r_f = r0_ref[sq, 0, 0, hd] if has_state else jnp.zeros(shape, F32)
            for c in range(n_chunks):
                rfs[c] = r_f.astype(BF16)
                r_f = gc_f * r_f + kvs[c][:RET_DK, :]
            r_b = r0_ref[sq, 0, 1, hd] if has_state else jnp.zeros(shape, F32)
            for c in reversed(range(n_chunks)):
                rbs[c] = r_b.astype(BF16)
                r_b = gc_b * r_b + kvs[c][RET_DK:, :]
            if emit_state:
                for s in range(rout_ref.shape[1]):
                    rout_ref[sq, s, 0, hd] = r_f if s == 0 else jnp.zeros(shape, F32)
                    rout_ref[sq, s, 1, hd] = r_b if s == 0 else jnp.zeros(shape, F32)

            for c in range(n_chunks):
                q = q_ref[rows_of(c), cols].astype(BF16)
                k = (k_ref[rows_of(c), cols] * k_scale).astype(BF16)
                att = lax.dot_general(q, k, _NT, preferred_element_type=F32) * dec
                o = jnp.dot(att.astype(BF16), v_ref[rows_of(c), cols].astype(BF16), preferred_element_type=F32)
                use_f = has_state or c > 0
                use_b = has_state or c < n_chunks - 1
                if use_f and use_b:
                    inter = jnp.dot(q, jnp.concatenate([rfs[c], rbs[c]], axis=1), preferred_element_type=F32)
                    o = o + inter[:, :RET_DV] * xi_f + inter[:, RET_DV:] * xi_b
                elif use_f:
                    o = o + jnp.dot(q, rfs[c], preferred_element_type=F32) * xi_f
                elif use_b:
                    o = o + jnp.dot(q, rbs[c], preferred_element_type=F32) * xi_b
                mu = jnp.mean(o, axis=-1, keepdims=True)
                var = jnp.mean(jnp.square(o - mu), axis=-1, keepdims=True)
                y = (o - mu) * lax.rsqrt(var + EPS) * g_ref[:, cols]
                o_ref[rows_of(c), cols] = (gp_ref[rows_of(c), cols] * y).astype(BF16)


def _retention(z, decay_b, g_ret_l, state_ret, l, group, stack=None):
    seq = SEQ if group == 0 else DEC_SEQ
    nb = BATCH if group == 0 else DEC_BATCH
    heads = RET_HEADS if group == 0 else 2
    n_seq = 2 if group == 0 else 1
    hblk = RET_HEADS // heads
    b0 = 0 if group == 0 else N_P // (n_seq * seq)
    has_state = group == 1
    emit_state = group == 0
    block = 2 * RET_CHUNK
    n_chunks = seq // block
    width = heads * LANE

    def zspec(zb):
        return pl.BlockSpec((n_seq * seq, width), lambda b, hb: (b0 + b, zb * Z_BLK // width + hb))

    in_specs = [zspec(ZB_RQ), zspec(ZB_RK), zspec(ZB_RV), zspec(ZB_GP_RET),
                pl.BlockSpec((1, 2, heads, 8, LANE), lambda b, hb: (l, 0, hb, 0, 0)),
                pl.BlockSpec((1, width), lambda b, hb: (0, hb))]
    args = [z, z, z, z, decay_b, g_ret_l]
    state_block = (n_seq, 1, 2, heads, RET_DK, RET_DV)
    if has_state:
        in_specs.append(pl.BlockSpec(state_block, lambda b, hb: (b, l, 0, hb, 0, 0)))
        args.append(state_ret)
    out_specs = [pl.BlockSpec((n_seq * seq, width), lambda b, hb: (b, hb))]
    out_shape = [jax.ShapeDtypeStruct((nb * seq, RET_W), BF16)]
    aliases = {}
    if emit_state:
        assert (l == 0) == (stack is None)
        slices = DEPTH if stack is None else 1
        out_specs.append(pl.BlockSpec((n_seq, slices) + state_block[2:], lambda b, hb: (b, l, 0, hb, 0, 0)))
        out_shape.append(jax.ShapeDtypeStruct((nb, DEPTH, 2, RET_HEADS, RET_DK, RET_DV), F32))
        if stack is not None:
            aliases = {len(args): 1}
            in_specs.append(pl.BlockSpec(memory_space=pl.ANY))
            args.append(stack)
    return pl.pallas_call(
        functools.partial(_ret_kernel, block=block, n_chunks=n_chunks, n_seq=n_seq, heads=heads,
                          has_state=has_state, emit_state=emit_state),
        grid=(nb // n_seq, hblk),
        in_specs=in_specs,
        out_specs=out_specs,
        out_shape=out_shape,
        input_output_aliases=aliases,
        compiler_params=_params(2),
        name="ret_sample" if has_state else "ret_prompt",
    )(*args)


def _tail_kernel(ap_m, as_m, a_sgu, ap_r, as_r, m_ref, wm_ref, ws_ref, wr_ref, wo_ref,
                 gpost_ref, mod_ref, *rest, tiles_p, first, last, layer):
    i = pl.program_id(0)
    is_p = i < tiles_p
    rest = list(rest)
    sem = rest.pop()
    stage = rest.pop()
    w_bf = [rest.pop() for _ in range(4)][::-1]

    @pl.when(i == 0)
    def _():
        rows_per_copy = stage.shape[1]
        jobs = [(src, dst, r0) for src, dst in zip((wm_ref, ws_ref, wr_ref, wo_ref), w_bf)
                for r0 in range(0, dst.shape[0], rows_per_copy)]

        def copy(k):
            src, _, r0 = jobs[k]
            return pltpu.make_async_copy(src.at[layer, pl.ds(r0, rows_per_copy), :], stage.at[k % 2], sem.at[k % 2])

        copy(0).start()
        for k, (_, dst, r0) in enumerate(jobs):
            if k + 1 < len(jobs):
                copy(k + 1).start()
            copy(k).wait()
            dst[r0:r0 + rows_per_copy, :] = stage[k % 2].astype(BF16)

    wm_ref, ws_ref, wr_ref, wo_ref = w_bf
    if first:
        x = jnp.where(is_p, rest.pop(0)[...], rest.pop(0)[...])
    else:
        x = rest.pop(0)[...]
    a_mla = jnp.where(is_p, ap_m[...], as_m[...])
    a_ret = jnp.where(is_p, ap_r[...], as_r[...])
    y_mla = jnp.dot(a_mla, wm_ref[...], preferred_element_type=F32)
    y = m_ref[:, :D_MODEL] * y_mla
    y_sgu = jnp.dot(a_sgu[...], ws_ref[...], preferred_element_type=F32)
    y = y + m_ref[:, D_MODEL:2 * D_MODEL] * y_sgu
    y_ret = jnp.dot(a_ret, wr_ref[...], preferred_element_type=F32)
    y = y + m_ref[:, 2 * D_MODEL:] * y_ret
    out = jnp.dot(y.astype(BF16), wo_ref[...], preferred_element_type=F32)
    x_new = x + mod_ref[0, 2:3, :] * _rms(out, gpost_ref[...])
    if last:
        yp_ref, ys_ref = rest

        @pl.when(is_p)
        def _():
            yp_ref[...] = x_new

        @pl.when(jnp.logical_not(is_p))
        def _():
            ys_ref[...] = x_new
    else:
        gpre_ref, modn_ref, x_out, h_out = rest
        x_out[...] = x_new
        h_out[...] = _prenorm_value(x_new, gpre_ref[...], modn_ref)


def _tail(a_mla_p, a_mla_s, a_sgu, a_ret_p, a_ret_s, z, w_br_mla, w_br_sgu, w_br_ret, w_out, xs, g_post_l,
          g_pre_next, mod3, l):
    tm = 256
    tiles_p = N_P // tm
    last = g_pre_next is None
    first = len(xs) == 2

    def p_spec():
        return pl.BlockSpec((tm, 1024), lambda i: (jnp.minimum(i, tiles_p - 1), 0))

    def s_spec():
        return pl.BlockSpec((tm, 1024), lambda i: (jnp.maximum(i - tiles_p, 0), 0))

    def w_spec(rows):
        del rows
        return pl.BlockSpec(memory_space=pl.ANY)

    def row_spec():
        return pl.BlockSpec((tm, D_MODEL), lambda i: (i, 0))

    in_specs = [p_spec(), s_spec(), pl.BlockSpec((tm, 1024), lambda i: (i, 0)), p_spec(), s_spec()]
    in_specs += [pl.BlockSpec((pl.Element(tm), pl.Element(N_BRANCH * D_MODEL)),
                              lambda i: (pl.multiple_of(i * tm, tm), ZB_MERGE * Z_BLK))]
    in_specs += [w_spec(1024), w_spec(1024), w_spec(1024), w_spec(D_MODEL),
                 pl.BlockSpec((1, D_MODEL), lambda i: (0, 0)), _mod_spec(l, tm)]
    args = [a_mla_p, a_mla_s, a_sgu, a_ret_p, a_ret_s, z]
    args += [w_br_mla, w_br_sgu, w_br_ret, w_out, g_post_l, mod3]
    if first:
        in_specs += [pl.BlockSpec((tm, D_MODEL), lambda i: (jnp.minimum(i, tiles_p - 1), 0)),
                     pl.BlockSpec((tm, D_MODEL), lambda i: (jnp.maximum(i - tiles_p, 0), 0))]
    else:
        in_specs += [row_spec()]
    args += list(xs)
    if last:
        out_specs = [pl.BlockSpec((tm, D_MODEL), lambda i: (jnp.minimum(i, tiles_p - 1), 0)),
                     pl.BlockSpec((tm, D_MODEL), lambda i: (jnp.maximum(i - tiles_p, 0), 0))]
        out_shape = [jax.ShapeDtypeStruct((N_P, D_MODEL), F32), jax.ShapeDtypeStruct((N_S, D_MODEL), F32)]
    else:
        in_specs += [pl.BlockSpec((1, D_MODEL), lambda i: (0, 0)), _mod_spec(l + 1, tm)]
        args += [g_pre_next, mod3]
        out_specs = [row_spec(), row_spec()]
        out_shape = [jax.ShapeDtypeStruct((N_TOK, D_MODEL), F32), jax.ShapeDtypeStruct((N_TOK, D_MODEL), BF16)]
    return pl.pallas_call(
        functools.partial(_tail_kernel, tiles_p=tiles_p, first=first, last=last, layer=l),
        grid=(N_TOK // tm,),
        in_specs=in_specs,
        out_specs=out_specs,
        out_shape=out_shape,
        scratch_shapes=[pltpu.VMEM((1024, D_MODEL), BF16), pltpu.VMEM((1024, D_MODEL), BF16),
                        pltpu.VMEM((1024, D_MODEL), BF16), pltpu.VMEM((D_MODEL, D_MODEL), BF16),
                        pltpu.VMEM((2, TAIL_STAGE_ROWS, D_MODEL), F32), pltpu.SemaphoreType.DMA((2,))],
        compiler_params=_params(1),
        name="tail_last" if last else "tail",
    )(*args)


def _rope_tables():
    f32 = np.float32
    pos = np.arange(DEC_SEQ)
    row = (pos // GRID_W).astype(f32)
    col = (pos % GRID_W).astype(f32)
    n_freq = QK_ROPE // 4
    inv = f32(ROPE_BASE) ** (-np.arange(n_freq, dtype=f32) / f32(n_freq))
    ang_r = row[:, None] * inv
    ang_c = col[:, None] * inv
    zeros = np.zeros((DEC_SEQ, n_freq), f32)
    pad = np.zeros((DEC_SEQ, LANE - QK_ROPE), f32)
    cr, sr, cc, sc = np.cos(ang_r), np.sin(ang_r), np.cos(ang_c), np.sin(ang_c)
    c = np.concatenate([cr, cr, cc, cc, pad], axis=1)
    s1 = np.concatenate([-sr, zeros, -sc, zeros, pad], axis=1)
    s2 = np.concatenate([zeros, sr, zeros, sc, pad], axis=1)
    return tuple(jnp.asarray(t.astype(f32)) for t in (c, s1, s2))


def kernel(x_prompt, x_sample, cache_ckv, cache_krope, state_ret, c, c_ctx, w_mod, b_mod, g_pre, g_post, w_in,
           g_q, g_kv, w_uq, w_ukv, g_sgu, w_sgu, b_sgu, ret_decay, g_ret, w_br_mla, w_br_sgu, w_br_ret, w_out):
    cond8 = jnp.concatenate([c_ctx[None, :], c, jnp.zeros((COND_ROWS - 1 - DEC_BATCH, D_MODEL), F32)], axis=0)
    mod3 = _modulation(cond8, w_mod, b_mod).reshape(DEPTH * COND_ROWS, 3, D_MODEL)
    w_in_t = jnp.swapaxes(w_in, 1, 2)
    w_kr = jnp.pad(w_in_t[:, KR_COL:KR_COL + QK_ROPE, :], ((0, 0), (0, LANE - QK_ROPE), (0, 0))).astype(BF16)
    w_uq_pad = jnp.pad(w_uq.reshape(DEPTH, Q_LORA, MLA_HEADS, QK_NOPE + QK_ROPE),
                       ((0, 0), (0, 0), (0, 0), (0, HEAD_SLOT - QK_NOPE - QK_ROPE))
                       ).reshape(DEPTH, Q_LORA, MLA_HEADS * HEAD_SLOT).astype(BF16)
    w_ukv_bf = w_ukv.astype(BF16)
    w_sgu_bf = w_sgu.astype(BF16)
    w_br = [w_br_mla, w_br_sgu, w_br_ret]
    w_out_bf = w_out
    b_sgu_exp = jnp.repeat(b_sgu, LANE, axis=2)
    decay_b = jnp.broadcast_to(ret_decay[:, :, :, None, None], (DEPTH, 2, RET_HEADS, 8, LANE))
    rope_tabs = _rope_tables()
    kc, vc = _ctxkv(cache_ckv, cache_krope, w_ukv_bf)

    xs = (x_prompt.reshape(N_P, D_MODEL), x_sample.reshape(N_S, D_MODEL))
    h = _prenorm(*xs, g_pre[0][None, :], mod3)
    cache_stacks = ret_stack = None
    for l in range(DEPTH):
        z = _inproj(h, w_in_t, l)
        q_p, k_p, v_p, *cache_stacks = _qkv(z, h, w_kr, w_uq_pad, w_ukv_bf, g_q[l][None, :], g_kv[l][None, :], None,
                                            l, 0, cache_stacks)
        a_mla_p = _attn_prompt(q_p, k_p, v_p, z)
        q_s, k_s, v_s = _qkv(z, h, w_kr, w_uq_pad, w_ukv_bf, g_q[l][None, :], g_kv[l][None, :], rope_tabs, l, 1)
        a_mla_s = _attn_sample(q_s, k_s, v_s, kc, vc, z, l)
        a_sgu = _sgu(z, g_sgu[l][None, :], w_sgu_bf, b_sgu_exp, l)
        a_ret_p, ret_stack = _retention(z, decay_b, g_ret[l][None, :], None, l, 0, ret_stack)
        (a_ret_s,) = _retention(z, decay_b, g_ret[l][None, :], state_ret, l, 1)
        g_pre_next = g_pre[l + 1][None, :] if l + 1 < DEPTH else None
        outs = _tail(a_mla_p, a_mla_s, a_sgu, a_ret_p, a_ret_s, z, *w_br, w_out_bf, xs, g_post[l][None, :],
                     g_pre_next, mod3, l)
        if l + 1 < DEPTH:
            x, h = outs
            xs = (x,)
        else:
            y_p, y_s = outs
    return (y_p.reshape(BATCH, SEQ, D_MODEL), y_s.reshape(DEC_BATCH, DEC_SEQ, D_MODEL), *cache_stacks, ret_stack)
```
